```python
import math
import jax, jax.numpy as jnp
from jax import lax
import numpy as np

D_MODEL = 1024
BATCH = 4
SEQ = 4096
DEPTH = 2
DEC_BATCH = 8
DEC_SEQ = 64
PAST_LEN = 4096

CHUNK = 64
N_MIXERS = 2
N_GLA = (DEPTH + 1) // 2
N_CONV = DEPTH // 2
GLA_HEADS = 4
GLA_DK = D_MODEL // 2 // GLA_HEADS
GLA_DV = D_MODEL // GLA_HEADS
GLA_KEY_DIM = GLA_HEADS * GLA_DK
GLA_VAL_DIM = GLA_HEADS * GLA_DV
GATE_RANK = 16
GATE_TAU = 16.0
CONV_WIDTH = 31
CONV_DIM = D_MODEL
FFN_DIM = 4 * D_MODEL
N_MOD = 6
EPS = 1e-6

kernel_name = "hybrid_gla_conformer_stream_step"


def rmsnorm(x, g):
    xf = x.astype(jnp.float32)
    y = xf * lax.rsqrt(jnp.mean(xf * xf, axis=-1, keepdims=True) + EPS)
    return (y * g.astype(jnp.float32)).astype(x.dtype)


def layernorm(x, g, b):
    xf = x.astype(jnp.float32)
    mu = jnp.mean(xf, axis=-1, keepdims=True)
    var = jnp.mean(jnp.square(xf - mu), axis=-1, keepdims=True)
    y = (xf - mu) * lax.rsqrt(var + EPS) * g.astype(jnp.float32) + b.astype(jnp.float32)
    return y.astype(x.dtype)


def gla_chunk_scan(q, k, v, g, s0):
    B, T = q.shape[0], q.shape[1]
    C = min(CHUNK, T)
    n = T // C

    def blocks(a):
        return a.reshape(B, n, C, GLA_HEADS, a.shape[-1]).transpose(1, 0, 3, 2, 4)

    qb, kb, vb, gb = blocks(q), blocks(k), blocks(v), blocks(g)
    bcum = jnp.cumsum(gb, axis=3)
    blast = bcum[:, :, :, -1:, :]
    q_dec = qb * jnp.exp(bcum)
    k_inv = kb * jnp.exp(-bcum)
    k_to_end = kb * jnp.exp(blast - bcum)
    mask = jnp.tril(jnp.ones((C, C), dtype=bool))
    scores = jnp.einsum('nbhcd,nbhsd->nbhcs', q_dec, k_inv)
    intra = jnp.einsum('nbhcs,nbhse->nbhce', jnp.where(mask, scores, 0.0), vb)

    def step(s, xs):
        qd, ke, vv, bl = xs
        inter = jnp.einsum('bhcd,bhde->bhce', qd, s)
        s_new = jnp.exp(bl[:, :, 0, :])[..., None] * s + jnp.einsum('bhcd,bhce->bhde', ke, vv)
        return s_new, inter

    s_fin, inter = lax.scan(step, s0, (q_dec, k_to_end, vb, blast))
    o = (intra + inter).transpose(1, 0, 3, 2, 4).reshape(B, T, GLA_HEADS, GLA_DV)
    return o, s_fin


def gla_mixer(h, s0, w_in, w_ga, w_gb, b_g, norm_g, w_out):
    B, T, _ = h.shape
    proj = h @ w_in
    q, k, v, r = jnp.split(proj, [GLA_KEY_DIM, 2 * GLA_KEY_DIM, 2 * GLA_KEY_DIM + GLA_VAL_DIM], axis=-1)
    g = jax.nn.log_sigmoid(((h @ w_ga) @ w_gb + b_g).astype(jnp.float32)) / GATE_TAU
    qh = q.astype(jnp.float32).reshape(B, T, GLA_HEADS, GLA_DK) * (GLA_DK ** -0.5)
    kh = k.astype(jnp.float32).reshape(B, T, GLA_HEADS, GLA_DK)
    vh = v.astype(jnp.float32).reshape(B, T, GLA_HEADS, GLA_DV)
    gh = g.reshape(B, T, GLA_HEADS, GLA_DK)
    o, s_fin = gla_chunk_scan(qh, kh, vh, gh, s0.astype(jnp.float32))
    o = rmsnorm(o, norm_g).reshape(B, T, GLA_VAL_DIM).astype(h.dtype)
    o = o * jax.nn.silu(r)
    return o @ w_out, s_fin


def conv_mixer(h, buf, w_in, b_in, w_dw, b_dw, ln_g, ln_b, w_out, b_out):
    u = h @ w_in + b_in
    a, gt = jnp.split(u, 2, axis=-1)
    u = a * jax.nn.sigmoid(gt)
    ext = jnp.concatenate([buf.astype(u.dtype), u], axis=1)
    y = lax.conv_general_dilated(ext, w_dw[:, None, :].astype(ext.dtype), window_strides=(1,),
                                 padding='VALID', dimension_numbers=('NWC', 'WIO', 'NWC'),
                                 feature_group_count=CONV_DIM) + b_dw
    y = jax.nn.silu(layernorm(y, ln_g, ln_b))
    return y @ w_out + b_out, ext[:, -(CONV_WIDTH - 1):]


def run_trunk(x, c, gla_states, conv_bufs, p):
    new_gla = []
    new_conv = []
    for i in range(DEPTH):
        mod = jax.nn.silu(c) @ p['w_mod'][i] + p['b_mod'][i]
        sh_m, sc_m, gt_m, sh_f, sc_f, gt_f = [m[:, None, :] for m in jnp.split(mod, N_MOD, axis=-1)]
        h = rmsnorm(x, p['norm_mix_pre'][i]) * (1.0 + sc_m) + sh_m
        j = i // N_MIXERS
        if i % N_MIXERS == 0:
            y, s = gla_mixer(h, gla_states[j], p['gla_w_in'][j], p['gla_w_gate_a'][j], p['gla_w_gate_b'][j],
                             p['gla_b_gate'][j], p['gla_norm'][j], p['gla_w_out'][j])
            new_gla.append(s)
        else:
            y, s = conv_mixer(h, conv_bufs[j], p['conv_w_in'][j], p['conv_b_in'][j], p['conv_w_dw'][j],
                              p['conv_b_dw'][j], p['conv_ln_g'][j], p['conv_ln_b'][j],
                              p['conv_w_out'][j], p['conv_b_out'][j])
            new_conv.append(s)
        x = x + gt_m * rmsnorm(y, p['norm_mix_post'][i])
        h = rmsnorm(x, p['norm_ffn_pre'][i]) * (1.0 + sc_f) + sh_f
        y = jnp.square(jax.nn.relu(h @ p['w_ffn_up'][i])) @ p['w_ffn_down'][i]
        x = x + gt_f * rmsnorm(y, p['norm_ffn_post'][i])
    return x, jnp.stack(new_gla), jnp.stack(new_conv)


def setup_inputs(seed: int = 0) -> dict:
    key = jax.random.key(seed)
    ks = jax.random.split(key, 32)

    def nrm(k, shape, scale):
        return jax.random.normal(k, shape, jnp.float32) * scale

    def gain(k, shape):
        return 1.0 + 0.05 * jax.random.normal(k, shape, jnp.float32)

    D = D_MODEL
    return {
        'x_prompt': nrm(ks[0], (BATCH, SEQ, D), 1.0),
        'x_sample': nrm(ks[1], (DEC_BATCH, DEC_SEQ, D), 1.0),
        'c_prompt': nrm(ks[2], (BATCH, D), 1.0),
        'c_sample': nrm(ks[3], (DEC_BATCH, D), 1.0),
        'state_gla': nrm(ks[4], (N_GLA, DEC_BATCH, GLA_HEADS, GLA_DK, GLA_DV), 0.5),
        'cache_conv': nrm(ks[5], (N_CONV, DEC_BATCH, CONV_WIDTH - 1, CONV_DIM), 1.0),
        'w_mod': nrm(ks[6], (DEPTH, D, N_MOD * D), 0.5 * D ** -0.5),
        'b_mod': nrm(ks[7], (DEPTH, N_MOD * D), 0.02),
        'norm_mix_pre': gain(ks[8], (DEPTH, D)),
        'norm_mix_post': gain(ks[9], (DEPTH, D)),
        'norm_ffn_pre': gain(ks[10], (DEPTH, D)),
        'norm_ffn_post': gain(ks[11], (DEPTH, D)),
        'w_ffn_up': nrm(ks[12], (DEPTH, D, FFN_DIM), D ** -0.5),
        'w_ffn_down': nrm(ks[13], (DEPTH, FFN_DIM, D), FFN_DIM ** -0.5),
        'gla_w_in': nrm(ks[14], (N_GLA, D, 2 * GLA_KEY_DIM + 2 * GLA_VAL_DIM), D ** -0.5),
        'gla_w_gate_a': nrm(ks[15], (N_GLA, D, GATE_RANK), D ** -0.5),
        'gla_w_gate_b': nrm(ks[16], (N_GLA, GATE_RANK, GLA_KEY_DIM), GATE_RANK ** -0.5),
        'gla_b_gate': nrm(ks[17], (N_GLA, GLA_KEY_DIM), 0.1),
        'gla_norm': gain(ks[18], (N_GLA, GLA_DV)),
        'gla_w_out': nrm(ks[19], (N_GLA, GLA_VAL_DIM, D), GLA_VAL_DIM ** -0.5),
        'conv_w_in': nrm(ks[20], (N_CONV, D, 2 * CONV_DIM), D ** -0.5),
        'conv_b_in': nrm(ks[21], (N_CONV, 2 * CONV_DIM), 0.02),
        'conv_w_dw': nrm(ks[22], (N_CONV, CONV_WIDTH, CONV_DIM), CONV_WIDTH ** -0.5),
        'conv_b_dw': nrm(ks[23], (N_CONV, CONV_DIM), 0.02),
        'conv_ln_g': gain(ks[24], (N_CONV, CONV_DIM)),
        'conv_ln_b': nrm(ks[25], (N_CONV, CONV_DIM), 0.02),
        'conv_w_out': nrm(ks[26], (N_CONV, CONV_DIM, D), CONV_DIM ** -0.5),
        'conv_b_out': nrm(ks[27], (N_CONV, D), 0.02),
    }


def reference(x_prompt, x_sample, c_prompt, c_sample, state_gla, cache_conv, w_mod, b_mod,
              norm_mix_pre, norm_mix_post, norm_ffn_pre, norm_ffn_post, w_ffn_up, w_ffn_down,
              gla_w_in, gla_w_gate_a, gla_w_gate_b, gla_b_gate, gla_norm, gla_w_out,
              conv_w_in, conv_b_in, conv_w_dw, conv_b_dw, conv_ln_g, conv_ln_b, conv_w_out, conv_b_out):
    p = {
        'w_mod': w_mod, 'b_mod': b_mod,
        'norm_mix_pre': norm_mix_pre, 'norm_mix_post': norm_mix_post,
        'norm_ffn_pre': norm_ffn_pre, 'norm_ffn_post': norm_ffn_post,
        'w_ffn_up': w_ffn_up, 'w_ffn_down': w_ffn_down,
        'gla_w_in': gla_w_in, 'gla_w_gate_a': gla_w_gate_a, 'gla_w_gate_b': gla_w_gate_b,
        'gla_b_gate': gla_b_gate, 'gla_norm': gla_norm, 'gla_w_out': gla_w_out,
        'conv_w_in': conv_w_in, 'conv_b_in': conv_b_in, 'conv_w_dw': conv_w_dw, 'conv_b_dw': conv_b_dw,
        'conv_ln_g': conv_ln_g, 'conv_ln_b': conv_ln_b, 'conv_w_out': conv_w_out, 'conv_b_out': conv_b_out,
    }
    b_p = x_prompt.shape[0]
    gla0 = jnp.zeros((N_GLA, b_p, GLA_HEADS, GLA_DK, GLA_DV), jnp.float32)
    conv0 = jnp.zeros((N_CONV, b_p, CONV_WIDTH - 1, CONV_DIM), x_prompt.dtype)
    y_prompt, new_gla_p, new_conv_p = run_trunk(x_prompt, c_prompt, gla0, conv0, p)
    y_sample, new_gla_s, new_conv_s = run_trunk(x_sample, c_sample, state_gla, cache_conv, p)
    return (y_prompt, y_sample, new_gla_p, new_conv_p, new_gla_s, new_conv_s)
```

```python
import functools

import jax
import jax.numpy as jnp
from jax import lax
from jax.experimental import pallas as pl
from jax.experimental.pallas import tpu as pltpu

F32 = jnp.float32
BF16 = jnp.bfloat16

CHUNK = 64
GLA_HEADS = 4
GATE_TAU = 16.0
CONV_WIDTH = 31
N_MOD = 6
EPS = 1e-6

SUBLANES = 8
HIST_ROWS = 32
HIST_PAD = HIST_ROWS - (CONV_WIDTH - 1)
CONV_ROW_BLOCK = 16
ROW_TILE = 512
VMEM_LIMIT_BYTES = 56 * 1024 * 1024
MOD_ROWS = 16
MOD_COL_TILE = 1536


def _sigmoid(x):
    return 1.0 / (1.0 + jnp.exp(-x))


def _silu(x):
    return x * _sigmoid(x)


def _log_sigmoid(x):
    return jnp.minimum(x, 0.0) - jnp.log(1.0 + jnp.exp(-jnp.abs(x)))


def _rms(x, g):
    return x * lax.rsqrt(jnp.mean(x * x, axis=-1, keepdims=True) + EPS) * g


def _dot(a, b):
    return jnp.dot(a, b, preferred_element_type=F32)


def _modulated_norm(x3, g, shift, scale):
    bb, tm, d = x3.shape
    h = _rms(x3.reshape(bb * tm, d), g).reshape(bb, tm, d) * (1.0 + scale) + shift
    return h.reshape(bb * tm, d).astype(BF16)


def _gated_residual(x3, y2, g, gate):
    bb, tm, d = x3.shape
    return x3 + gate * _rms(y2, g).reshape(bb, tm, d)


def _mod_kernel(c_ref, w_ref, b_ref, o_ref):
    s = _silu(c_ref[...]).astype(BF16)
    o_ref[0] = _dot(s, w_ref[0].astype(BF16)) + b_ref[0]


def _modulation(c_all, w_mod, b_mod):
    depth, d, n = w_mod.shape
    return pl.pallas_call(
        _mod_kernel,
        grid=(depth, n // MOD_COL_TILE),
        in_specs=[
            pl.BlockSpec((MOD_ROWS, d), lambda i, j: (0, 0)),
            pl.BlockSpec((1, d, MOD_COL_TILE), lambda i, j: (i, 0, j)),
            pl.BlockSpec((1, 1, MOD_COL_TILE), lambda i, j: (i, 0, j)),
        ],
        out_specs=pl.BlockSpec((1, MOD_ROWS, MOD_COL_TILE), lambda i, j: (i, 0, j)),
        out_shape=jax.ShapeDtypeStruct((depth, MOD_ROWS, n), F32),
        compiler_params=pltpu.CompilerParams(
            dimension_semantics=("arbitrary", "arbitrary"), vmem_limit_bytes=VMEM_LIMIT_BYTES),
        name="modulation",
    )(c_all, w_mod, b_mod.reshape(depth, 1, n))


def _gla_kernel(x_ref, mod_ref, s0_ref, npre_ref, win_ref, wga_ref, wgb_ref, bg_ref, gn_ref, wout_ref,
                npost_ref, xo_ref, st_ref, proj_scr, g_scr, o_scr, *, bb, tm):
    d = x_ref.shape[-1]
    kdim = wgb_ref.shape[-1]
    dk = kdim // GLA_HEADS
    vdim = wout_ref.shape[0]
    dv = vdim // GLA_HEADS
    n_chunks = tm // CHUNK

    @pl.when(pl.program_id(1) == 0)
    def _():
        st_ref[...] = s0_ref[...]

    x3 = x_ref[...]
    m = mod_ref[...]
    hb = _modulated_norm(x3, npre_ref[...], m[:, 0:1, :], m[:, 1:2, :])
    proj_scr[...] = _dot(hb, win_ref[...]).reshape(bb, tm, -1)
    ga = _dot(hb, wga_ref[...]).astype(BF16)
    gl = _dot(ga, wgb_ref[...]) + bg_ref[...]
    g_scr[...] = (_log_sigmoid(gl) / GATE_TAU).reshape(bb, tm, kdim)

    row = lax.broadcasted_iota(jnp.int32, (CHUNK, CHUNK), 0)
    col = lax.broadcasted_iota(jnp.int32, (CHUNK, CHUNK), 1)
    causal = col <= row
    tri = causal.astype(F32)
    gn = gn_ref[...]
    q_scale = dk ** -0.5

    def chunk_step(idx, carry):
        b = idx // n_chunks
        r0 = pl.multiple_of((idx % n_chunks) * CHUNK, CHUNK)
        rows = pl.ds(r0, CHUNK)
        q = proj_scr[b, rows, 0:kdim]
        k = proj_scr[b, rows, kdim:2 * kdim]
        bcum = jnp.dot(tri, g_scr[b, rows, :], precision=lax.Precision.HIGHEST, preferred_element_type=F32)
        blast = bcum[CHUNK - 1:CHUNK, :]
        q_dec = (q * q_scale) * jnp.exp(bcum)
        k_inv = k * jnp.exp(-bcum)
        k_end = k * jnp.exp(blast - bcum)
        decay_col = jnp.broadcast_to(jnp.exp(blast), (SUBLANES, kdim)).T
        for h in range(GLA_HEADS):
            ks = slice(h * dk, (h + 1) * dk)
            vs = slice(2 * kdim + h * dv, 2 * kdim + (h + 1) * dv)
            qh = q_dec[:, ks].astype(BF16)
            kih = k_inv[:, ks].astype(BF16)
            keh_t = k_end[:, ks].T.astype(BF16)
            vh = proj_scr[b, rows, vs].astype(BF16)
            scores = lax.dot_general(qh, kih, (((1,), (1,)), ((), ())), preferred_element_type=F32)
            scores = jnp.where(causal, scores, 0.0).astype(BF16)
            s = st_ref[b, h]
            o = _dot(scores, vh) + _dot(qh, s.astype(BF16))
            st_ref[b, h] = decay_col[ks, 0:1] * s + _dot(keh_t, vh)
            o_scr[b, rows, h * dv:(h + 1) * dv] = _rms(o, gn)
        return carry

    lax.fori_loop(0, bb * n_chunks, chunk_step, 0)

    r = proj_scr[:, :, 2 * kdim + vdim:]
    z = (o_scr[...] * _silu(r)).reshape(bb * tm, vdim).astype(BF16)
    y = _dot(z, wout_ref[...])
    xo_ref[...] = _gated_residual(x3, y, npost_ref[...], m[:, 2:3, :])


def _const_spec(shape):
    nd = len(shape)
    return pl.BlockSpec(shape, lambda i, j: (0,) * nd)


def _gla_layer(x, mod, s0, npre, w_in, w_ga, w_gb, b_g, gnorm, w_out, npost, *, bb, tm):
    bsz, t, d = x.shape
    _, heads, dk, dv = s0.shape
    pdim = w_in.shape[1]
    kdim, vdim = heads * dk, heads * dv
    grid = (bsz // bb, t // tm)
    xspec = pl.BlockSpec((bb, tm, d), lambda i, j: (i, j, 0))
    sspec = pl.BlockSpec((bb, heads, dk, dv), lambda i, j: (i, 0, 0, 0))
    return pl.pallas_call(
        functools.partial(_gla_kernel, bb=bb, tm=tm),
        grid=grid,
        in_specs=[
            xspec,
            pl.BlockSpec((bb, N_MOD, d), lambda i, j: (i, 0, 0)),
            sspec,
            _const_spec((1, d)),
            _const_spec((d, pdim)),
            _const_spec(w_ga.shape),
            _const_spec(w_gb.shape),
            _const_spec((1, kdim)),
            _const_spec((1, dv)),
            _const_spec((vdim, d)),
            _const_spec((1, d)),
        ],
        out_specs=[xspec, sspec],
        out_shape=[jax.ShapeDtypeStruct(x.shape, F32), jax.ShapeDtypeStruct(s0.shape, F32)],
        scratch_shapes=[
            pltpu.VMEM((bb, tm, pdim), F32),
            pltpu.VMEM((bb, tm, kdim), F32),
            pltpu.VMEM((bb, tm, vdim), F32),
        ],
        compiler_params=pltpu.CompilerParams(
            dimension_semantics=("arbitrary", "arbitrary"), vmem_limit_bytes=VMEM_LIMIT_BYTES),
        name="gla_mixer",
    )(x, mod, s0, npre.reshape(1, d), w_in, w_ga, w_gb, b_g.reshape(1, kdim), gnorm.reshape(1, dv), w_out,
      npost.reshape(1, d))


def _conv_kernel(x_ref, mod_ref, cache_ref, npre_ref, win_ref, bin_ref, wdw_ref, bdw_ref, lng_ref, lnb_ref,
                 wout_ref, bout_ref, npost_ref, xo_ref, cache_out_ref, ext_scr, shift_scr, y_scr, *, bb, tm):
    c = wdw_ref.shape[-1]
    ext_rows = HIST_ROWS + tm

    @pl.when(pl.program_id(1) == 0)
    def _():
        ext_scr[:, 0:HIST_ROWS, :] = cache_ref[...]
        ext_scr[:, ext_rows:, :] = jnp.zeros((bb, SUBLANES, c), F32)

    x3 = x_ref[...]
    m = mod_ref[...]
    hb = _modulated_norm(x3, npre_ref[...], m[:, 0:1, :], m[:, 1:2, :])
    u = _dot(hb, win_ref[...]) + bin_ref[...]
    glu = u[:, :c] * _sigmoid(u[:, c:])
    ext_scr[:, HIST_ROWS:ext_rows, :] = glu.reshape(bb, tm, c)

    n_shift_blocks = ext_rows // SUBLANES

    def shift_step(idx, carry):
        b = idx // n_shift_blocks
        r0 = pl.multiple_of((idx % n_shift_blocks) * SUBLANES, SUBLANES)
        two = ext_scr[b, pl.ds(r0, 2 * SUBLANES), :]
        for s in range(1, SUBLANES):
            shift_scr[s - 1, b, pl.ds(r0, SUBLANES), :] = pltpu.roll(two, 2 * SUBLANES - s, axis=0)[0:SUBLANES]
        return carry

    lax.fori_loop(0, bb * n_shift_blocks, shift_step, 0)

    n_row_blocks = tm // CONV_ROW_BLOCK

    def tap_step(idx, carry):
        b = idx // n_row_blocks
        r0 = pl.multiple_of((idx % n_row_blocks) * CONV_ROW_BLOCK, CONV_ROW_BLOCK)
        acc = jnp.zeros((CONV_ROW_BLOCK, c), F32)
        for j in range(CONV_WIDTH):
            off = HIST_PAD + j
            aligned, s = (off // SUBLANES) * SUBLANES, off % SUBLANES
            rows = pl.ds(r0 + aligned, CONV_ROW_BLOCK)
            src = ext_scr[b, rows, :] if s == 0 else shift_scr[s - 1, b, rows, :]
            acc = acc + wdw_ref[j:j + 1, :] * src
        y_scr[b, pl.ds(r0, CONV_ROW_BLOCK), :] = acc
        return carry

    lax.fori_loop(0, bb * n_row_blocks, tap_step, 0)

    tail = ext_scr[:, tm:ext_rows, :]
    cache_out_ref[...] = tail
    ext_scr[:, 0:HIST_ROWS, :] = tail

    y = y_scr[...].reshape(bb * tm, c) + bdw_ref[...]
    mu = jnp.mean(y, axis=-1, keepdims=True)
    yc = y - mu
    var = jnp.mean(yc * yc, axis=-1, keepdims=True)
    z = _silu(yc * lax.rsqrt(var + EPS) * lng_ref[...] + lnb_ref[...]).astype(BF16)
    out = _dot(z, wout_ref[...]) + bout_ref[...]
    xo_ref[...] = _gated_residual(x3, out, npost_ref[...], m[:, 2:3, :])


def _conv_layer(x, mod, cache, npre, w_in, b_in, w_dw, b_dw, ln_g, ln_b, w_out, b_out, npost, *, bb, tm):
    bsz, t, d = x.shape
    c = w_dw.shape[-1]
    grid = (bsz // bb, t // tm)
    xspec = pl.BlockSpec((bb, tm, d), lambda i, j: (i, j, 0))
    cspec = pl.BlockSpec((bb, HIST_ROWS, c), lambda i, j: (i, 0, 0))
    cache32 = jnp.pad(cache, ((0, 0), (HIST_PAD, 0), (0, 0)))
    x_new, cache_new = pl.pallas_call(
        functools.partial(_conv_kernel, bb=bb, tm=tm),
        grid=grid,
        in_specs=[
            xspec,
            pl.BlockSpec((bb, N_MOD, d), lambda i, j: (i, 0, 0)),
            cspec,
            _const_spec((1, d)),
            _const_spec(w_in.shape),
            _const_spec((1, 2 * c)),
            _const_spec(w_dw.shape),
            _const_spec((1, c)),
            _const_spec((1, c)),
            _const_spec((1, c)),
            _const_spec(w_out.shape),
            _const_spec((1, d)),
            _const_spec((1, d)),
        ],
        out_specs=[xspec, cspec],
        out_shape=[jax.ShapeDtypeStruct(x.shape, F32), jax.ShapeDtypeStruct((bsz, HIST_ROWS, c), F32)],
        scratch_shapes=[
            pltpu.VMEM((bb, HIST_ROWS + tm + SUBLANES, c), F32),
            pltpu.VMEM((SUBLANES - 1, bb, HIST_ROWS + tm, c), F32),
            pltpu.VMEM((bb, tm, c), F32),
        ],
        compiler_params=pltpu.CompilerParams(
            dimension_semantics=("arbitrary", "arbitrary"), vmem_limit_bytes=VMEM_LIMIT_BYTES),
        name="conv_mixer",
    )(x, mod, cache32, npre.reshape(1, d), w_in, b_in.reshape(1, 2 * c), w_dw, b_dw.reshape(1, c),
      ln_g.reshape(1, c), ln_b.reshape(1, c), w_out, b_out.reshape(1, d), npost.reshape(1, d))
    return x_new, cache_new[:, HIST_PAD:, :]


def _ffn_kernel(x_ref, mod_ref, npre_ref, wup_ref, wdn_ref, npost_ref, xo_ref, *, ff_tile):
    x3 = x_ref[...]
    bb, tm, d = x3.shape
    m = mod_ref[...]
    hb = _modulated_norm(x3, npre_ref[...], m[:, 3:4, :], m[:, 4:5, :])
    y = jnp.zeros((bb * tm, d), F32)
    for f0 in range(0, wup_ref.shape[1], ff_tile):
        up = _dot(hb, wup_ref[:, f0:f0 + ff_tile])
        y = y + _dot(jnp.square(jnp.maximum(up, 0.0)).astype(BF16), wdn_ref[f0:f0 + ff_tile, :])
    xo_ref[...] = _gated_residual(x3, y, npost_ref[...], m[:, 5:6, :])


def _ffn_layer(x, mod, npre, w_up, w_dn, npost, *, bb, tm):
    bsz, t, d = x.shape
    xspec = pl.BlockSpec((bb, tm, d), lambda i, j: (i, j, 0))
    return pl.pallas_call(
        functools.partial(_ffn_kernel, ff_tile=1024),
        grid=(bsz // bb, t // tm),
        in_specs=[
            xspec,
            pl.BlockSpec((bb, N_MOD, d), lambda i, j: (i, 0, 0)),
            _const_spec((1, d)),
            _const_spec(w_up.shape),
            _const_spec(w_dn.shape),
            _const_spec((1, d)),
        ],
        out_specs=xspec,
        out_shape=jax.ShapeDtypeStruct(x.shape, F32),
        compiler_params=pltpu.CompilerParams(
            dimension_semantics=("arbitrary", "arbitrary"), vmem_limit_bytes=VMEM_LIMIT_BYTES),
        name="ffn",
    )(x, mod, npre.reshape(1, d), w_up, w_dn, npost.reshape(1, d))


def _tiling(bsz, t):
    tm = min(t, ROW_TILE)
    bb = max(1, min(bsz, ROW_TILE // tm))
    assert t % tm == 0 and bsz % bb == 0 and tm % CHUNK == 0
    return bb, tm


def _run_trunk(x, mod, gla_states, conv_caches, p):
    bb, tm = _tiling(x.shape[0], x.shape[1])
    depth = p['w_ffn_up'].shape[0]
    new_gla, new_conv = [], []
    for i in range(depth):
        j = i // 2
        if i % 2 == 0:
            x, s = _gla_layer(x, mod[i], gla_states[j], p['norm_mix_pre'][i], p['gla_w_in'][j], p['gla_w_gate_a'][j],
                              p['gla_w_gate_b'][j], p['gla_b_gate'][j], p['gla_norm'][j], p['gla_w_out'][j],
                              p['norm_mix_post'][i], bb=bb, tm=tm)
            new_gla.append(s)
        else:
            x, s = _conv_layer(x, mod[i], conv_caches[j], p['norm_mix_pre'][i], p['conv_w_in'][j], p['conv_b_in'][j],
                               p['conv_w_dw'][j], p['conv_b_dw'][j], p['conv_ln_g'][j], p['conv_ln_b'][j],
                               p['conv_w_out'][j], p['conv_b_out'][j], p['norm_mix_post'][i], bb=bb, tm=tm)
            new_conv.append(s)
        x = _ffn_layer(x, mod[i], p['norm_ffn_pre'][i], p['w_ffn_up'][i], p['w_ffn_down'][i], p['norm_ffn_post'][i],
                       bb=bb, tm=tm)
    return x, jnp.stack(new_gla), jnp.stack(new_conv)


def kernel(x_prompt, x_sample, c_prompt, c_sample, state_gla, cache_conv, w_mod, b_mod, norm_mix_pre, norm_mix_post, norm_ffn_pre, norm_ffn_post, w_ffn_up, w_ffn_down, gla_w_in, gla_w_gate_a, gla_w_gate_b, gla_b_gate, gla_norm, gla_w_out, conv_w_in, conv_b_in, conv_w_dw, conv_b_dw, conv_ln_g, conv_ln_b, conv_w_out, conv_b_out):
    p = {
        'norm_mix_pre': norm_mix_pre, 'norm_mix_post': norm_mix_post,
        'norm_ffn_pre': norm_ffn_pre, 'norm_ffn_post': norm_ffn_post,
        'w_ffn_up': w_ffn_up.astype(BF16), 'w_ffn_down': w_ffn_down.astype(BF16),
        'gla_w_in': gla_w_in.astype(BF16), 'gla_w_gate_a': gla_w_gate_a.astype(BF16),
        'gla_w_gate_b': gla_w_gate_b.astype(BF16), 'gla_b_gate': gla_b_gate, 'gla_norm': gla_norm,
        'gla_w_out': gla_w_out.astype(BF16),
        'conv_w_in': conv_w_in.astype(BF16), 'conv_b_in': conv_b_in, 'conv_w_dw': conv_w_dw, 'conv_b_dw': conv_b_dw,
        'conv_ln_g': conv_ln_g, 'conv_ln_b': conv_ln_b, 'conv_w_out': conv_w_out.astype(BF16),
        'conv_b_out': conv_b_out,
    }
    depth, d, _ = w_mod.shape
    b_p, b_s = x_prompt.shape[0], x_sample.shape[0]
    n_gla, _, heads, dk, dv = state_gla.shape
    n_conv, _, hist, cdim = cache_conv.shape

    c_all = jnp.concatenate([c_prompt, c_sample, jnp.zeros((MOD_ROWS - b_p - b_s, d), F32)], axis=0)
    mod = _modulation(c_all, w_mod, b_mod)
    mod_p = mod[:, :b_p].reshape(depth, b_p, N_MOD, d)
    mod_s = mod[:, b_p:b_p + b_s].reshape(depth, b_s, N_MOD, d)

    gla0 = jnp.zeros((n_gla, b_p, heads, dk, dv), F32)
    conv0 = jnp.zeros((n_conv, b_p, hist, cdim), F32)
    y_p, gla_p, conv_p = _run_trunk(x_prompt, mod_p, gla0, conv0, p)
    y_s, gla_s, conv_s = _run_trunk(x_sample, mod_s, state_gla, cache_conv, p)
    return (y_p, y_s, gla_p, conv_p, gla_s, conv_s)
```

```python
import functools

import jax
import jax.numpy as jnp
from jax import lax
from jax.experimental import pallas as pl
from jax.experimental.pallas import tpu as pltpu

F32 = jnp.float32
BF16 = jnp.bfloat16

CHUNK = 64
GLA_HEADS = 4
GATE_TAU = 16.0
CONV_WIDTH = 31
N_MOD = 6
EPS = 1e-6

SUBLANES = 8
HIST_ROWS = 32
HIST_PAD = HIST_ROWS - (CONV_WIDTH - 1)
CONV_ROW_BLOCK = 32
FF_TILES = 8
ROW_TILE = 512
VMEM_LIMIT_BYTES = 56 * 1024 * 1024
MOD_ROWS = 16
MOD_COL_TILE = 1536


def _sigmoid(x):
    return 1.0 / (1.0 + jnp.exp(-x))


def _silu(x):
    return x * _sigmoid(x)


def _log_sigmoid(x):
    return jnp.minimum(x, 0.0) - jnp.log(1.0 + jnp.exp(-jnp.abs(x)))


def _rms(x, g):
    return x * lax.rsqrt(jnp.mean(x * x, axis=-1, keepdims=True) + EPS) * g


def _layer_norm_silu(y, g, b):
    mu = jnp.mean(y, axis=-1, keepdims=True)
    yc = y - mu
    var = jnp.mean(yc * yc, axis=-1, keepdims=True)
    return _silu(yc * lax.rsqrt(var + EPS) * g + b)


def _dot(a, b):
    return jnp.dot(a, b, preferred_element_type=F32)


def _modulated_norm(x3, g, shift, scale):
    bb, tm, d = x3.shape
    h = _rms(x3.reshape(bb * tm, d), g).reshape(bb, tm, d) * (1.0 + scale) + shift
    return h.reshape(bb * tm, d).astype(BF16)


def _gated_residual(x3, y2, g, gate):
    bb, tm, d = x3.shape
    return x3 + gate * _rms(y2, g).reshape(bb, tm, d)


def _const_spec(shape):
    nd = len(shape)
    return pl.BlockSpec(shape, lambda i, j: (0,) * nd)


def _resident_spec(shape):
    nd = len(shape)
    return pl.BlockSpec(shape, lambda *_: (0,) * nd, pipeline_mode=pl.Buffered(1))


def _mod_kernel(c_ref, w_ref, b_ref, o_ref):
    s = _silu(c_ref[...]).astype(BF16)
    o_ref[0] = _dot(s, w_ref[0].astype(BF16)) + b_ref[0]


def _modulation(c_all, w_mod, b_mod):
    depth, d, n = w_mod.shape
    return pl.pallas_call(
        _mod_kernel,
        grid=(depth, n // MOD_COL_TILE),
        in_specs=[
            pl.BlockSpec((MOD_ROWS, d), lambda i, j: (0, 0)),
            pl.BlockSpec((1, d, MOD_COL_TILE), lambda i, j: (i, 0, j)),
            pl.BlockSpec((1, 1, MOD_COL_TILE), lambda i, j: (i, 0, j)),
        ],
        out_specs=pl.BlockSpec((1, MOD_ROWS, MOD_COL_TILE), lambda i, j: (i, 0, j)),
        out_shape=jax.ShapeDtypeStruct((depth, MOD_ROWS, n), F32),
        compiler_params=pltpu.CompilerParams(
            dimension_semantics=("arbitrary", "arbitrary"), vmem_limit_bytes=VMEM_LIMIT_BYTES),
        name="modulation",
    )(c_all, w_mod, b_mod.reshape(depth, 1, n))


def _gla_kernel(x_ref, mod_ref, s0_ref, npre_ref, win_ref, wga_ref, wgb_ref, bg_ref, gn_ref, wout_ref,
                npost_ref, xo_ref, st_ref, proj_scr, g_scr, o_scr, *, bb, tm):
    d = x_ref.shape[-1]
    kdim = wgb_ref.shape[-1]
    dk = kdim // GLA_HEADS
    vdim = wout_ref.shape[0]
    dv = vdim // GLA_HEADS
    n_chunks = tm // CHUNK

    @pl.when(pl.program_id(1) == 0)
    def _():
        st_ref[...] = s0_ref[...]

    x3 = x_ref[...]
    m = mod_ref[...]
    hb = _modulated_norm(x3, npre_ref[...], m[:, 0:1, :], m[:, 1:2, :])
    proj_scr[...] = _dot(hb, win_ref[...]).reshape(bb, tm, -1)
    ga = _dot(hb, wga_ref[...]).astype(BF16)
    gl = _dot(ga, wgb_ref[...]) + bg_ref[...]
    g_scr[...] = (_log_sigmoid(gl) / GATE_TAU).reshape(bb, tm, kdim)

    row = lax.broadcasted_iota(jnp.int32, (CHUNK, CHUNK), 0)
    col = lax.broadcasted_iota(jnp.int32, (CHUNK, CHUNK), 1)
    causal = col <= row
    tri = causal.astype(F32)
    gn = gn_ref[...]
    q_scale = dk ** -0.5

    def chunk_step(idx, carry):
        b = idx // n_chunks
        r0 = pl.multiple_of((idx % n_chunks) * CHUNK, CHUNK)
        rows = pl.ds(r0, CHUNK)
        q = proj_scr[b, rows, 0:kdim]
        k = proj_scr[b, rows, kdim:2 * kdim]
        bcum = jnp.dot(tri, g_scr[b, rows, :], precision=lax.Precision.HIGHEST, preferred_element_type=F32)
        blast = bcum[CHUNK - 1:CHUNK, :]
        q_dec = (q * q_scale) * jnp.exp(bcum)
        k_inv = k * jnp.exp(-bcum)
        k_end = k * jnp.exp(blast - bcum)
        decay_col = jnp.broadcast_to(jnp.exp(blast), (SUBLANES, kdim)).T
        for h in range(GLA_HEADS):
            ks = slice(h * dk, (h + 1) * dk)
            vs = slice(2 * kdim + h * dv, 2 * kdim + (h + 1) * dv)
            qh = q_dec[:, ks].astype(BF16)
            kih = k_inv[:, ks].astype(BF16)
            keh_t = k_end[:, ks].T.astype(BF16)
            vh = proj_scr[b, rows, vs].astype(BF16)
            scores = lax.dot_general(qh, kih, (((1,), (1,)), ((), ())), preferred_element_type=F32)
            scores = jnp.where(causal, scores, 0.0).astype(BF16)
            s = st_ref[b, h]
            o = _dot(scores, vh) + _dot(qh, s.astype(BF16))
            st_ref[b, h] = decay_col[ks, 0:1] * s + _dot(keh_t, vh)
            o_scr[b, rows, h * dv:(h + 1) * dv] = _rms(o, gn)
        return carry

    lax.fori_loop(0, bb * n_chunks, chunk_step, 0)

    r = proj_scr[:, :, 2 * kdim + vdim:]
    z = (o_scr[...] * _silu(r)).reshape(bb * tm, vdim).astype(BF16)
    y = _dot(z, wout_ref[...])
    xo_ref[...] = _gated_residual(x3, y, npost_ref[...], m[:, 2:3, :])


def _gla_layer(x, mod, s0, npre, w_in, w_ga, w_gb, b_g, gnorm, w_out, npost, *, bb, tm):
    bsz, t, d = x.shape
    _, heads, dk, dv = s0.shape
    pdim = w_in.shape[1]
    kdim, vdim = heads * dk, heads * dv
    grid = (bsz // bb, t // tm)
    xspec = pl.BlockSpec((bb, tm, d), lambda i, j: (i, j, 0))
    sspec = pl.BlockSpec((bb, heads, dk, dv), lambda i, j: (i, 0, 0, 0))
    return pl.pallas_call(
        functools.partial(_gla_kernel, bb=bb, tm=tm),
        grid=grid,
        in_specs=[
            xspec,
            pl.BlockSpec((bb, N_MOD, d), lambda i, j: (i, 0, 0)),
            sspec,
            _const_spec((1, d)),
            _const_spec((d, pdim)),
            _const_spec(w_ga.shape),
            _const_spec(w_gb.shape),
            _const_spec((1, kdim)),
            _const_spec((1, dv)),
            _const_spec((vdim, d)),
            _const_spec((1, d)),
        ],
        out_specs=[xspec, sspec],
        out_shape=[jax.ShapeDtypeStruct(x.shape, F32), jax.ShapeDtypeStruct(s0.shape, F32)],
        scratch_shapes=[
            pltpu.VMEM((bb, tm, pdim), F32),
            pltpu.VMEM((bb, tm, kdim), F32),
            pltpu.VMEM((bb, tm, vdim), F32),
        ],
        compiler_params=pltpu.CompilerParams(
            dimension_semantics=("arbitrary", "arbitrary"), vmem_limit_bytes=VMEM_LIMIT_BYTES),
        name="gla_mixer",
    )(x, mod, s0, npre.reshape(1, d), w_in, w_ga, w_gb, b_g.reshape(1, kdim), gnorm.reshape(1, dv), w_out,
      npost.reshape(1, d))


def _ffn_convin_kernel(x_ref, mod_ref, mod_next_ref, npre_ref, wup_ref, wdn_ref, npost_ref, npre_next_ref, win_ref,
                       bin_ref, xo_ref, u_ref):
    x3 = x_ref[...]
    bb, tm, d = x3.shape
    c = u_ref.shape[-1]
    m = mod_ref[...]
    hb = _modulated_norm(x3, npre_ref[...], m[:, 3:4, :], m[:, 4:5, :])
    y = jnp.zeros((bb * tm, d), F32)
    for f in range(wup_ref.shape[0]):
        up = _dot(hb, wup_ref[f])
        y = y + _dot(jnp.square(jnp.maximum(up, 0.0)).astype(BF16), wdn_ref[f])
    x1 = _gated_residual(x3, y, npost_ref[...], m[:, 5:6, :])
    xo_ref[...] = x1
    mn = mod_next_ref[...]
    hb1 = _modulated_norm(x1, npre_next_ref[...], mn[:, 0:1, :], mn[:, 1:2, :])
    u = _dot(hb1, win_ref[...]) + bin_ref[...]
    u_ref[...] = (u[:, :c] * _sigmoid(u[:, c:])).reshape(bb, tm, c)


def _ffn_convin_layer(x, mod, mod_next, npre, w_up, w_dn, npost, npre_next, w_in, b_in, *, bb, tm):
    bsz, t, d = x.shape
    c = w_in.shape[1] // 2
    xspec = pl.BlockSpec((bb, tm, d), lambda i, j: (i, j, 0))
    mspec = pl.BlockSpec((bb, N_MOD, d), lambda i, j: (i, 0, 0))
    return pl.pallas_call(
        _ffn_convin_kernel,
        grid=(bsz // bb, t // tm),
        in_specs=[
            xspec, mspec, mspec,
            _resident_spec((1, d)),
            _resident_spec(w_up.shape),
            _resident_spec(w_dn.shape),
            _resident_spec((1, d)),
            _resident_spec((1, d)),
            _resident_spec(w_in.shape),
            _resident_spec((1, 2 * c)),
        ],
        out_specs=[xspec, pl.BlockSpec((bb, tm, c), lambda i, j: (i, j, 0))],
        out_shape=[jax.ShapeDtypeStruct(x.shape, F32), jax.ShapeDtypeStruct((bsz, t, c), F32)],
        compiler_params=pltpu.CompilerParams(
            dimension_semantics=("arbitrary", "arbitrary"), vmem_limit_bytes=VMEM_LIMIT_BYTES),
        name="ffn_convin",
    )(x, mod, mod_next, npre.reshape(1, d), w_up, w_dn, npost.reshape(1, d), npre_next.reshape(1, d), w_in,
      b_in.reshape(1, 2 * c))


def _conv_ffn_kernel(u_ref, x_ref, mod_ref, cache_ref, wdw_ref, bdw_ref, lng_ref, lnb_ref, wout_ref, bout_ref,
                     npost_ref, npre_f_ref, wup_ref, wdn_ref, npost_f_ref, xo_ref, cache_out_ref,
                     ext_scr, shift_scr, wtap_scr, y_scr, hb_scr, acc_scr, *, bb, tm, n_tiles, tiles_per_stream):
    c = wdw_ref.shape[-1]
    n_ff = wup_ref.shape[0]
    ext_rows = HIST_ROWS + tm
    rows_per_iter = bb * tm // n_ff
    win_rows = rows_per_iter + HIST_ROWS
    n = pl.program_id(0)
    cur = jnp.minimum(n, n_tiles - 1)

    @pl.when(n == 0)
    def _():
        y_scr[...] = jnp.zeros(y_scr.shape, F32)
        ext_scr[:, ext_rows:, :] = jnp.zeros((bb, SUBLANES, c), F32)
        for j in range(CONV_WIDTH):
            wtap_scr[j * SUBLANES:(j + 1) * SUBLANES, :] = jnp.broadcast_to(wdw_ref[j:j + 1, :], (SUBLANES, c))

    @pl.when(cur % tiles_per_stream == 0)
    def _():
        ext_scr[:, 0:HIST_ROWS, :] = cache_ref[...]

    ext_scr[:, HIST_ROWS:ext_rows, :] = u_ref[...]

    m = mod_ref[...]
    z = _layer_norm_silu(y_scr[...].reshape(bb * tm, c) + bdw_ref[...], lng_ref[...], lnb_ref[...]).astype(BF16)
    out = _dot(z, wout_ref[...]) + bout_ref[...]
    x1 = _gated_residual(x_ref[...], out, npost_ref[...], m[:, 2:3, :])
    xo_ref[...] = x1
    hb_scr[...] = _modulated_norm(x1, npre_f_ref[...], m[:, 3:4, :], m[:, 4:5, :])
    acc_scr[...] = jnp.zeros(acc_scr.shape, F32)

    def step(k, carry):
        row = k * rows_per_iter
        b = row // tm
        r0 = pl.multiple_of(row % tm, rows_per_iter)
        for g in range(win_rows // SUBLANES):
            two = ext_scr[b, pl.ds(r0 + g * SUBLANES, 2 * SUBLANES), :]
            for s in range(1, SUBLANES):
                shift_scr[s - 1, g * SUBLANES:(g + 1) * SUBLANES, :] = (
                    pltpu.roll(two, 2 * SUBLANES - s, axis=0)[0:SUBLANES])
        for q0 in range(0, rows_per_iter, CONV_ROW_BLOCK):
            acc = jnp.zeros((CONV_ROW_BLOCK // SUBLANES, SUBLANES, c), F32)
            for j in range(CONV_WIDTH):
                off = HIST_PAD + j
                aligned, s = q0 + (off // SUBLANES) * SUBLANES, off % SUBLANES
                if s == 0:
                    src = ext_scr[b, pl.ds(r0 + aligned, CONV_ROW_BLOCK), :]
                else:
                    src = shift_scr[s - 1, aligned:aligned + CONV_ROW_BLOCK, :]
                w8 = wtap_scr[j * SUBLANES:(j + 1) * SUBLANES, :]
                acc = acc + w8[None] * src.reshape(acc.shape)
            y_scr[b, pl.ds(r0 + q0, CONV_ROW_BLOCK), :] = acc.reshape(CONV_ROW_BLOCK, c)
        up = _dot(hb_scr[...], wup_ref[k])
        acc_scr[...] += _dot(jnp.square(jnp.maximum(up, 0.0)).astype(BF16), wdn_ref[k])
        return carry

    lax.fori_loop(0, n_ff, step, 0)

    xo_ref[...] = _gated_residual(xo_ref[...], acc_scr[...], npost_f_ref[...], m[:, 5:6, :])
    tail = ext_scr[:, tm:ext_rows, :]
    cache_out_ref[...] = tail
    ext_scr[:, 0:HIST_ROWS, :] = tail


def _conv_ffn_layer(u, x, mod, cache, w_dw, b_dw, ln_g, ln_b, w_out, b_out, npost, npre_f, w_up, w_dn, npost_f,
                    *, bb, tm):
    bsz, t, d = x.shape
    c = w_dw.shape[-1]
    n_ff = w_up.shape[0]
    tps = t // tm
    n_tiles = (bsz // bb) * tps
    rows_per_iter = bb * tm // n_ff
    assert tm % rows_per_iter == 0 and rows_per_iter % CONV_ROW_BLOCK == 0 and tm >= HIST_ROWS

    def cur(n):
        return jnp.minimum(n, n_tiles - 1)

    def prev(n):
        return jnp.maximum(n - 1, 0)

    cache32 = jnp.pad(cache, ((0, 0), (HIST_PAD, 0), (0, 0)))
    cspec = pl.BlockSpec((bb, HIST_ROWS, c), lambda n: (cur(n) // tps, 0, 0))
    xspec = pl.BlockSpec((bb, tm, d), lambda n: (prev(n) // tps, prev(n) % tps, 0))
    x_new, cache_new = pl.pallas_call(
        functools.partial(_conv_ffn_kernel, bb=bb, tm=tm, n_tiles=n_tiles, tiles_per_stream=tps),
        grid=(n_tiles + 1,),
        in_specs=[
            pl.BlockSpec((bb, tm, c), lambda n: (cur(n) // tps, cur(n) % tps, 0)),
            xspec,
            pl.BlockSpec((bb, N_MOD, d), lambda n: (prev(n) // tps, 0, 0)),
            cspec,
            _resident_spec(w_dw.shape),
            _resident_spec((1, c)),
            _resident_spec((1, c)),
            _resident_spec((1, c)),
            _resident_spec(w_out.shape),
            _resident_spec((1, d)),
            _resident_spec((1, d)),
            _resident_spec((1, d)),
            _resident_spec(w_up.shape),
            _resident_spec(w_dn.shape),
            _resident_spec((1, d)),
        ],
        out_specs=[xspec, cspec],
        out_shape=[jax.ShapeDtypeStruct(x.shape, F32), jax.ShapeDtypeStruct((bsz, HIST_ROWS, c), F32)],
        scratch_shapes=[
            pltpu.VMEM((bb, HIST_ROWS + tm + SUBLANES, c), F32),
            pltpu.VMEM((SUBLANES - 1, rows_per_iter + HIST_ROWS, c), F32),
            pltpu.VMEM((CONV_WIDTH * SUBLANES, c), F32),
            pltpu.VMEM((bb, tm, c), F32),
            pltpu.VMEM((bb * tm, d), BF16),
            pltpu.VMEM((bb * tm, d), F32),
        ],
        compiler_params=pltpu.CompilerParams(
            dimension_semantics=("arbitrary",), vmem_limit_bytes=VMEM_LIMIT_BYTES),
        name="conv_ffn",
    )(u, x, mod, cache32, w_dw, b_dw.reshape(1, c), ln_g.reshape(1, c), ln_b.reshape(1, c), w_out,
      b_out.reshape(1, d), npost.reshape(1, d), npre_f.reshape(1, d), w_up, w_dn, npost_f.reshape(1, d))
    return x_new, cache_new[:, HIST_PAD:, :]


def _tiling(bsz, t):
    tm = min(t, ROW_TILE)
    bb = max(1, min(bsz, ROW_TILE // tm))
    assert t % tm == 0 and bsz % bb == 0 and tm % CHUNK == 0
    return bb, tm


def _ffn_tiles(w_up, w_dn):
    d, f = w_up.shape
    ft = f // FF_TILES
    return (w_up.astype(BF16).reshape(d, FF_TILES, ft).transpose(1, 0, 2), w_dn.astype(BF16).reshape(FF_TILES, ft, d))


def _run_trunk(x, mod, gla_states, conv_caches, p):
    bb, tm = _tiling(x.shape[0], x.shape[1])
    depth = p['w_ffn_up'].shape[0]
    assert depth % 2 == 0
    new_gla, new_conv = [], []
    for j in range(depth // 2):
        i0, i1 = 2 * j, 2 * j + 1
        x, s = _gla_layer(x, mod[i0], gla_states[j], p['norm_mix_pre'][i0], p['gla_w_in'][j], p['gla_w_gate_a'][j],
                          p['gla_w_gate_b'][j], p['gla_b_gate'][j], p['gla_norm'][j], p['gla_w_out'][j],
                          p['norm_mix_post'][i0], bb=bb, tm=tm)
        new_gla.append(s)
        up0, dn0 = _ffn_tiles(p['w_ffn_up'][i0], p['w_ffn_down'][i0])
        x, u = _ffn_convin_layer(x, mod[i0], mod[i1], p['norm_ffn_pre'][i0], up0, dn0, p['norm_ffn_post'][i0],
                                 p['norm_mix_pre'][i1], p['conv_w_in'][j], p['conv_b_in'][j], bb=bb, tm=tm)
        up1, dn1 = _ffn_tiles(p['w_ffn_up'][i1], p['w_ffn_down'][i1])
        x, s = _conv_ffn_layer(u, x, mod[i1], conv_caches[j], p['conv_w_dw'][j], p['conv_b_dw'][j], p['conv_ln_g'][j],
                               p['conv_ln_b'][j], p['conv_w_out'][j], p['conv_b_out'][j], p['norm_mix_post'][i1],
                               p['norm_ffn_pre'][i1], up1, dn1, p['norm_ffn_post'][i1], bb=bb, tm=tm)
        new_conv.append(s)
    return x, jnp.stack(new_gla), jnp.stack(new_conv)


def kernel(x_prompt, x_sample, c_prompt, c_sample, state_gla, cache_conv, w_mod, b_mod, norm_mix_pre, norm_mix_post, norm_ffn_pre, norm_ffn_post, w_ffn_up, w_ffn_down, gla_w_in, gla_w_gate_a, gla_w_gate_b, gla_b_gate, gla_norm, gla_w_out, conv_w_in, conv_b_in, conv_w_dw, conv_b_dw, conv_ln_g, conv_ln_b, conv_w_out, conv_b_out):
    p = {
        'norm_mix_pre': norm_mix_pre, 'norm_mix_post': norm_mix_post,
        'norm_ffn_pre': norm_ffn_pre, 'norm_ffn_post': norm_ffn_post,
        'w_ffn_up': w_ffn_up, 'w_ffn_down': w_ffn_down,
        'gla_w_in': gla_w_in.astype(BF16), 'gla_w_gate_a': gla_w_gate_a.astype(BF16),
        'gla_w_gate_b': gla_w_gate_b.astype(BF16), 'gla_b_gate': gla_b_gate, 'gla_norm': gla_norm,
        'gla_w_out': gla_w_out.astype(BF16),
        'conv_w_in': conv_w_in.astype(BF16), 'conv_b_in': conv_b_in, 'conv_w_dw': conv_w_dw, 'conv_b_dw': conv_b_dw,
        'conv_ln_g': conv_ln_g, 'conv_ln_b': conv_ln_b, 'conv_w_out': conv_w_out.astype(BF16),
        'conv_b_out': conv_b_out,
    }
    depth, d, _ = w_mod.shape
    b_p, b_s = x_prompt.shape[0], x_sample.shape[0]
    n_gla, _, heads, dk, dv = state_gla.shape
    n_conv, _, hist, cdim = cache_conv.shape

    c_all = jnp.concatenate([c_prompt, c_sample, jnp.zeros((MOD_ROWS - b_p - b_s, d), F32)], axis=0)
    mod = _modulation(c_all, w_mod, b_mod)
    mod_p = mod[:, :b_p].reshape(depth, b_p, N_MOD, d)
    mod_s = mod[:, b_p:b_p + b_s].reshape(depth, b_s, N_MOD, d)

    gla0 = jnp.zeros((n_gla, b_p, heads, dk, dv), F32)
    conv0 = jnp.zeros((n_conv, b_p, hist, cdim), F32)
    y_p, gla_p, conv_p = _run_trunk(x_prompt, mod_p, gla0, conv0, p)
    y_s, gla_s, conv_s = _run_trunk(x_sample, mod_s, state_gla, cache_conv, p)
    return (y_p, y_s, gla_p, conv_p, gla_s, conv_s)
```

```python
import functools

import jax
import jax.numpy as jnp
from jax import lax
from jax.experimental import pallas as pl
from jax.experimental.pallas import tpu as pltpu

F32 = jnp.float32
BF16 = jnp.bfloat16

CHUNK = 64
GLA_HEADS = 4
GATE_TAU = 16.0
CONV_WIDTH = 31
N_MOD = 6
EPS = 1e-6

SUBLANES = 8
HIST_ROWS = 32
HIST_PAD = HIST_ROWS - (CONV_WIDTH - 1)
CONV_ROW_BLOCK = 32
FF_TILE = 1024
ROW_TILE = 512
VMEM_LIMIT_BYTES = 56 * 1024 * 1024
MOD_ROWS = 16
MOD_COL_TILE = 1536


def _sigmoid(x):
    return 1.0 / (1.0 + jnp.exp(-x))


def _silu(x):
    return x * _sigmoid(x)


def _log_sigmoid(x):
    return jnp.minimum(x, 0.0) - jnp.log(1.0 + jnp.exp(-jnp.abs(x)))


def _rms(x, g):
    return x * lax.rsqrt(jnp.mean(x * x, axis=-1, keepdims=True) + EPS) * g


def _layer_norm_silu(y, g, b):
    mu = jnp.mean(y, axis=-1, keepdims=True)
    yc = y - mu
    var = jnp.mean(yc * yc, axis=-1, keepdims=True)
    return _silu(yc * lax.rsqrt(var + EPS) * g + b)


def _dot(a, b):
    return jnp.dot(a, b, preferred_element_type=F32)


def _split3(x):
    hi = x.astype(BF16)
    r1 = x - hi.astype(F32)
    mid = r1.astype(BF16)
    lo = (r1 - mid.astype(F32)).astype(BF16)
    return hi, mid, lo


def _modulated_norm(x3, g, shift, scale):
    bb, tm, d = x3.shape
    h = _rms(x3.reshape(bb * tm, d), g).reshape(bb, tm, d) * (1.0 + scale) + shift
    return h.reshape(bb * tm, d).astype(BF16)


def _gated_residual(x3, y2, g, gate):
    bb, tm, d = x3.shape
    return x3 + gate * _rms(y2, g).reshape(bb, tm, d)


def _resident_spec(shape):
    nd = len(shape)
    return pl.BlockSpec(shape, lambda *_: (0,) * nd, pipeline_mode=pl.Buffered(1))


_GRID2 = pltpu.CompilerParams(dimension_semantics=("arbitrary", "arbitrary"), vmem_limit_bytes=VMEM_LIMIT_BYTES)


def _mod_kernel(c_ref, w_ref, b_ref, o_ref):
    s = _silu(c_ref[...]).astype(BF16)
    o_ref[0] = _dot(s, w_ref[0].astype(BF16)) + b_ref[0]


def _modulation(c_all, w_mod, b_mod):
    depth, d, n = w_mod.shape
    return pl.pallas_call(
        _mod_kernel,
        grid=(depth, n // MOD_COL_TILE),
        in_specs=[
            pl.BlockSpec((MOD_ROWS, d), lambda i, j: (0, 0)),
            pl.BlockSpec((1, d, MOD_COL_TILE), lambda i, j: (i, 0, j)),
            pl.BlockSpec((1, 1, MOD_COL_TILE), lambda i, j: (i, 0, j)),
        ],
        out_specs=pl.BlockSpec((1, MOD_ROWS, MOD_COL_TILE), lambda i, j: (i, 0, j)),
        out_shape=jax.ShapeDtypeStruct((depth, MOD_ROWS, n), F32),
        compiler_params=_GRID2,
        name="modulation",
    )(c_all, w_mod, b_mod.reshape(depth, 1, n))


def _gla_kernel(x_ref, mod_ref, s0_ref, npre_ref, win_ref, wga_ref, wgb_ref, bg_ref, gn_ref, wout_ref,
                npost_ref, xo_ref, st_ref, proj_scr, g_scr, o_scr, *, bb, tm):
    kdim = wgb_ref.shape[-1]
    dk = kdim // GLA_HEADS
    vdim = wout_ref.shape[0]
    dv = vdim // GLA_HEADS
    n_chunks = tm // CHUNK

    @pl.when(pl.program_id(1) == 0)
    def _():
        st_ref[...] = s0_ref[...]

    x3 = x_ref[...]
    m = mod_ref[...]
    hb = _modulated_norm(x3, npre_ref[...], m[:, 0:1, :], m[:, 1:2, :])
    proj_scr[...] = _dot(hb, win_ref[...]).reshape(bb, tm, -1)
    ga = _dot(hb, wga_ref[...]).astype(BF16)
    gl = _dot(ga, wgb_ref[...]) + bg_ref[...]
    g_scr[...] = (_log_sigmoid(gl) / GATE_TAU).reshape(bb, tm, kdim)

    row = lax.broadcasted_iota(jnp.int32, (CHUNK, CHUNK), 0)
    col = lax.broadcasted_iota(jnp.int32, (CHUNK, CHUNK), 1)
    causal = col <= row
    tri = causal.astype(BF16)
    gn = gn_ref[...]
    q_scale = dk ** -0.5

    def chunk_step(b, c):
        rows = slice(c * CHUNK, (c + 1) * CHUNK)
        q = proj_scr[b, rows, 0:kdim]
        k = proj_scr[b, rows, kdim:2 * kdim]
        g_hi, g_mid, g_lo = _split3(g_scr[b, rows, :])
        bcum = _dot(tri, g_hi) + _dot(tri, g_mid) + _dot(tri, g_lo)
        blast = bcum[CHUNK - 1:CHUNK, :]
        q_dec = (q * q_scale) * jnp.exp(bcum)
        k_inv = k * jnp.exp(-bcum)
        k_end = k * jnp.exp(blast - bcum)
        decay_col = jnp.broadcast_to(jnp.exp(blast), (SUBLANES, kdim)).T
        for h in range(GLA_HEADS):
            ks = slice(h * dk, (h + 1) * dk)
            vs = slice(2 * kdim + h * dv, 2 * kdim + (h + 1) * dv)
            qh = q_dec[:, ks].astype(BF16)
            kih = k_inv[:, ks].astype(BF16)
            keh_t = k_end[:, ks].T.astype(BF16)
            vh = proj_scr[b, rows, vs].astype(BF16)
            scores = lax.dot_general(qh, kih, (((1,), (1,)), ((), ())), preferred_element_type=F32)
            scores = jnp.where(causal, scores, 0.0).astype(BF16)
            both = _dot(jnp.concatenate([scores, keh_t], axis=0), vh)
            s = st_ref[b, h]
            o = both[:CHUNK] + _dot(qh, s.astype(BF16))
            st_ref[b, h] = decay_col[ks, 0:1] * s + both[CHUNK:]
            o_scr[b, rows, h * dv:(h + 1) * dv] = _rms(o, gn)

    for b in range(bb):
        for c in range(n_chunks):
            chunk_step(b, c)

    r = proj_scr[:, :, 2 * kdim + vdim:]
    z = (o_scr[...] * _silu(r)).reshape(bb * tm, vdim).astype(BF16)
    y = _dot(z, wout_ref[...])
    xo_ref[...] = _gated_residual(x3, y, npost_ref[...], m[:, 2:3, :])


def _gla_layer(x, mod, s0, npre, w_in, w_ga, w_gb, b_g, gnorm, w_out, npost, *, bb, tm):
    bsz, t, d = x.shape
    _, heads, dk, dv = s0.shape
    pdim = w_in.shape[1]
    kdim, vdim = heads * dk, heads * dv
    xspec = pl.BlockSpec((bb, tm, d), lambda i, j: (i, j, 0))
    sspec = pl.BlockSpec((bb, heads, dk, dv), lambda i, j: (i, 0, 0, 0))
    return pl.pallas_call(
        functools.partial(_gla_kernel, bb=bb, tm=tm),
        grid=(bsz // bb, t // tm),
        in_specs=[
            xspec,
            pl.BlockSpec((bb, N_MOD, d), lambda i, j: (i, 0, 0)),
            sspec,
            _resident_spec((1, d)),
            _resident_spec((d, pdim)),
            _resident_spec(w_ga.shape),
            _resident_spec(w_gb.shape),
            _resident_spec((1, kdim)),
            _resident_spec((1, dv)),
            _resident_spec((vdim, d)),
            _resident_spec((1, d)),
        ],
        out_specs=[xspec, sspec],
        out_shape=[jax.ShapeDtypeStruct(x.shape, F32), jax.ShapeDtypeStruct(s0.shape, F32)],
        scratch_shapes=[
            pltpu.VMEM((bb, tm, pdim), F32),
            pltpu.VMEM((bb, tm, kdim), F32),
            pltpu.VMEM((bb, tm, vdim), F32),
        ],
        compiler_params=_GRID2,
        name="gla_mixer",
    )(x, mod, s0, npre.reshape(1, d), w_in, w_ga, w_gb, b_g.reshape(1, kdim), gnorm.reshape(1, dv), w_out,
      npost.reshape(1, d))


def _conv_kernel(x_ref, mod_ref, cache_ref, npre_ref, win_ref, bin_ref, wdw_ref, bdw_ref, lng_ref, lnb_ref,
                 wout_ref, bout_ref, npost_ref, xo_ref, cache_out_ref, ext_scr, shift_scr, wtap_scr, y_scr,
                 *, bb, tm):
    c = wdw_ref.shape[-1]
    ext_rows = HIST_ROWS + tm

    @pl.when(pl.program_id(1) == 0)
    def _():
        ext_scr[:, 0:HIST_ROWS, :] = cache_ref[...]
        ext_scr[:, ext_rows:, :] = jnp.zeros((bb, SUBLANES, c), F32)
        for j in range(CONV_WIDTH):
            wtap_scr[j * SUBLANES:(j + 1) * SUBLANES, :] = jnp.broadcast_to(wdw_ref[j:j + 1, :], (SUBLANES, c))

    x3 = x_ref[...]
    m = mod_ref[...]
    hb = _modulated_norm(x3, npre_ref[...], m[:, 0:1, :], m[:, 1:2, :])
    u = _dot(hb, win_ref[...]) + bin_ref[...]
    glu = u[:, :c] * _sigmoid(u[:, c:])
    ext_scr[:, HIST_ROWS:ext_rows, :] = glu.reshape(bb, tm, c)

    n_shift_blocks = ext_rows // SUBLANES

    def shift_step(idx, carry):
        b = idx // n_shift_blocks
        r0 = pl.multiple_of((idx % n_shift_blocks) * SUBLANES, SUBLANES)
        two = ext_scr[b, pl.ds(r0, 2 * SUBLANES), :]
        for s in range(1, SUBLANES):
            shift_scr[s - 1, b, pl.ds(r0, SUBLANES), :] = pltpu.roll(two, 2 * SUBLANES - s, axis=0)[0:SUBLANES]
        return carry

    lax.fori_loop(0, bb * n_shift_blocks, shift_step, 0)

    n_row_blocks = tm // CONV_ROW_BLOCK

    def tap_step(idx, carry):
        b = idx // n_row_blocks
        r0 = pl.multiple_of((idx % n_row_blocks) * CONV_ROW_BLOCK, CONV_ROW_BLOCK)
        acc = jnp.zeros((CONV_ROW_BLOCK // SUBLANES, SUBLANES, c), F32)
        for j in range(CONV_WIDTH):
            off = HIST_PAD + j
            aligned, s = (off // SUBLANES) * SUBLANES, off % SUBLANES
            rows = pl.ds(r0 + aligned, CONV_ROW_BLOCK)
            src = ext_scr[b, rows, :] if s == 0 else shift_scr[s - 1, b, rows, :]
            w8 = wtap_scr[j * SUBLANES:(j + 1) * SUBLANES, :]
            acc = acc + w8[None] * src.reshape(acc.shape)
        y_scr[b, pl.ds(r0, CONV_ROW_BLOCK), :] = acc.reshape(CONV_ROW_BLOCK, c)
        return carry

    lax.fori_loop(0, bb * n_row_blocks, tap_step, 0)

    tail = ext_scr[:, tm:ext_rows, :]
    cache_out_ref[...] = tail
    ext_scr[:, 0:HIST_ROWS, :] = tail

    z = _layer_norm_silu(y_scr[...].reshape(bb * tm, c) + bdw_ref[...], lng_ref[...], lnb_ref[...]).astype(BF16)
    out = _dot(z, wout_ref[...]) + bout_ref[...]
    xo_ref[...] = _gated_residual(x3, out, npost_ref[...], m[:, 2:3, :])


def _conv_layer(x, mod, cache, npre, w_in, b_in, w_dw, b_dw, ln_g, ln_b, w_out, b_out, npost, *, bb, tm):
    bsz, t, d = x.shape
    c = w_dw.shape[-1]
    assert tm % CONV_ROW_BLOCK == 0 and tm >= HIST_ROWS
    xspec = pl.BlockSpec((bb, tm, d), lambda i, j: (i, j, 0))
    cspec = pl.BlockSpec((bb, HIST_ROWS, c), lambda i, j: (i, 0, 0))
    cache32 = jnp.pad(cache, ((0, 0), (HIST_PAD, 0), (0, 0)))
    x_new, cache_new = pl.pallas_call(
        functools.partial(_conv_kernel, bb=bb, tm=tm),
        grid=(bsz // bb, t // tm),
        in_specs=[
            xspec,
            pl.BlockSpec((bb, N_MOD, d), lambda i, j: (i, 0, 0)),
            cspec,
            _resident_spec((1, d)),
            _resident_spec(w_in.shape),
            _resident_spec((1, 2 * c)),
            _resident_spec(w_dw.shape),
            _resident_spec((1, c)),
            _resident_spec((1, c)),
            _resident_spec((1, c)),
            _resident_spec(w_out.shape),
            _resident_spec((1, d)),
            _resident_spec((1, d)),
        ],
        out_specs=[xspec, cspec],
        out_shape=[jax.ShapeDtypeStruct(x.shape, F32), jax.ShapeDtypeStruct((bsz, HIST_ROWS, c), F32)],
        scratch_shapes=[
            pltpu.VMEM((bb, HIST_ROWS + tm + SUBLANES, c), F32),
            pltpu.VMEM((SUBLANES - 1, bb, HIST_ROWS + tm, c), F32),
            pltpu.VMEM((CONV_WIDTH * SUBLANES, c), F32),
            pltpu.VMEM((bb, tm, c), F32),
        ],
        compiler_params=_GRID2,
        name="conv_mixer",
    )(x, mod, cache32, npre.reshape(1, d), w_in, b_in.reshape(1, 2 * c), w_dw, b_dw.reshape(1, c),
      ln_g.reshape(1, c), ln_b.reshape(1, c), w_out, b_out.reshape(1, d), npost.reshape(1, d))
    return x_new, cache_new[:, HIST_PAD:, :]


def _ffn_kernel(x_ref, mod_ref, npre_ref, wup_ref, wdn_ref, npost_ref, xo_ref):
    x3 = x_ref[...]
    bb, tm, d = x3.shape
    m = mod_ref[...]
    hb = _modulated_norm(x3, npre_ref[...], m[:, 3:4, :], m[:, 4:5, :])
    y = jnp.zeros((bb * tm, d), F32)
    for f0 in range(0, wup_ref.shape[1], FF_TILE):
        up = _dot(hb, wup_ref[:, f0:f0 + FF_TILE])
        y = y + _dot(jnp.square(jnp.maximum(up, 0.0)).astype(BF16), wdn_ref[f0:f0 + FF_TILE, :])
    xo_ref[...] = _gated_residual(x3, y, npost_ref[...], m[:, 5:6, :])


def _ffn_layer(x, mod, npre, w_up, w_dn, npost, *, bb, tm):
    bsz, t, d = x.shape
    xspec = pl.BlockSpec((bb, tm, d), lambda i, j: (i, j, 0))
    return pl.pallas_call(
        _ffn_kernel,
        grid=(bsz // bb, t // tm),
        in_specs=[
            xspec,
            pl.BlockSpec((bb, N_MOD, d), lambda i, j: (i, 0, 0)),
            _resident_spec((1, d)),
            _resident_spec(w_up.shape),
            _resident_spec(w_dn.shape),
            _resident_spec((1, d)),
        ],
        out_specs=xspec,
        out_shape=jax.ShapeDtypeStruct(x.shape, F32),
        compiler_params=_GRID2,
        name="ffn",
    )(x, mod, npre.reshape(1, d), w_up, w_dn, npost.reshape(1, d))


def _tiling(bsz, t):
    tm = min(t, ROW_TILE)
    bb = max(1, min(bsz, ROW_TILE // tm))
    assert t % tm == 0 and bsz % bb == 0 and tm % CHUNK == 0
    return bb, tm


def _run_trunk(x, mod, gla_states, conv_caches, p):
    bb, tm = _tiling(x.shape[0], x.shape[1])
    depth = p['w_ffn_up'].shape[0]
    new_gla, new_conv = [], []
    for i in range(depth):
        j = i // 2
        if i % 2 == 0:
            x, s = _gla_layer(x, mod[i], gla_states[j], p['norm_mix_pre'][i], p['gla_w_in'][j], p['gla_w_gate_a'][j],
                              p['gla_w_gate_b'][j], p['gla_b_gate'][j], p['gla_norm'][j], p['gla_w_out'][j],
                              p['norm_mix_post'][i], bb=bb, tm=tm)
            new_gla.append(s)
        else:
            x, s = _conv_layer(x, mod[i], conv_caches[j], p['norm_mix_pre'][i], p['conv_w_in'][j], p['conv_b_in'][j],
                               p['conv_w_dw'][j], p['conv_b_dw'][j], p['conv_ln_g'][j], p['conv_ln_b'][j],
                               p['conv_w_out'][j], p['conv_b_out'][j], p['norm_mix_post'][i], bb=bb, tm=tm)
            new_conv.append(s)
        x = _ffn_layer(x, mod[i], p['norm_ffn_pre'][i], p['w_ffn_up'][i], p['w_ffn_down'][i], p['norm_ffn_post'][i],
                       bb=bb, tm=tm)
    return x, jnp.stack(new_gla), jnp.stack(new_conv)


def kernel(x_prompt, x_sample, c_prompt, c_sample, state_gla, cache_conv, w_mod, b_mod, norm_mix_pre, norm_mix_post, norm_ffn_pre, norm_ffn_post, w_ffn_up, w_ffn_down, gla_w_in, gla_w_gate_a, gla_w_gate_b, gla_b_gate, gla_norm, gla_w_out, conv_w_in, conv_b_in, conv_w_dw, conv_b_dw, conv_ln_g, conv_ln_b, conv_w_out, conv_b_out):
    p = {
        'norm_mix_pre': norm_mix_pre, 'norm_mix_post': norm_mix_post,
        'norm_ffn_pre': norm_ffn_pre, 'norm_ffn_post': norm_ffn_post,
        'w_ffn_up': w_ffn_up.astype(BF16), 'w_ffn_down': w_ffn_down.astype(BF16),
        'gla_w_in': gla_w_in.astype(BF16), 'gla_w_gate_a': gla_w_gate_a.astype(BF16),
        'gla_w_gate_b': gla_w_gate_b.astype(BF16), 'gla_b_gate': gla_b_gate, 'gla_norm': gla_norm,
        'gla_w_out': gla_w_out.astype(BF16),
        'conv_w_in': conv_w_in.astype(BF16), 'conv_b_in': conv_b_in, 'conv_w_dw': conv_w_dw, 'conv_b_dw': conv_b_dw,
        'conv_ln_g': conv_ln_g, 'conv_ln_b': conv_ln_b, 'conv_w_out': conv_w_out.astype(BF16),
        'conv_b_out': conv_b_out,
    }
    depth, d, _ = w_mod.shape
    b_p, b_s = x_prompt.shape[0], x_sample.shape[0]
    n_gla, _, heads, dk, dv = state_gla.shape
    n_conv, _, hist, cdim = cache_conv.shape

    c_all = jnp.concatenate([c_prompt, c_sample, jnp.zeros((MOD_ROWS - b_p - b_s, d), F32)], axis=0)
    mod = _modulation(c_all, w_mod, b_mod)
    mod_p = mod[:, :b_p].reshape(depth, b_p, N_MOD, d)
    mod_s = mod[:, b_p:b_p + b_s].reshape(depth, b_s, N_MOD, d)

    gla0 = jnp.zeros((n_gla, b_p, heads, dk, dv), F32)
    conv0 = jnp.zeros((n_conv, b_p, hist, cdim), F32)
    y_p, gla_p, conv_p = _run_trunk(x_prompt, mod_p, gla0, conv0, p)
    y_s, gla_s, conv_s = _run_trunk(x_sample, mod_s, state_gla, cache_conv, p)
    return (y_p, y_s, gla_p, conv_p, gla_s, conv_s)
```

```python
import functools

import jax
import jax.numpy as jnp
from jax import lax
from jax.experimental import pallas as pl
from jax.experimental.pallas import tpu as pltpu

F32 = jnp.float32
BF16 = jnp.bfloat16

CHUNK = 64
GLA_HEADS = 4
GATE_TAU = 16.0
CONV_WIDTH = 31
N_MOD = 6
EPS = 1e-6

SUBLANES = 8
HIST_ROWS = 32
HIST_PAD = HIST_ROWS - (CONV_WIDTH - 1)
CONV_ROW_BLOCK = 64
CONV_LANE_BLOCK = 128
SHIFT_ROW_BLOCK = 32
FF_TILE = 1024
ROW_TILE = 512
VMEM_LIMIT_BYTES = 56 * 1024 * 1024
MOD_ROWS = 16
MOD_COL_TILE = 1536


def _sigmoid(x):
    return 1.0 / (1.0 + jnp.exp(-x))


def _silu(x):
    return x * _sigmoid(x)


def _log_sigmoid(x):
    return jnp.minimum(x, 0.0) - jnp.log(1.0 + jnp.exp(-jnp.abs(x)))


def _rms(x, g):
    return x * lax.rsqrt(jnp.mean(x * x, axis=-1, keepdims=True) + EPS) * g


def _layer_norm_silu(y, g, b):
    mu = jnp.mean(y, axis=-1, keepdims=True)
    yc = y - mu
    var = jnp.mean(yc * yc, axis=-1, keepdims=True)
    return _silu(yc * lax.rsqrt(var + EPS) * g + b)


def _dot(a, b):
    return jnp.dot(a, b, preferred_element_type=F32)


def _modulated_norm(x3, g, shift, scale):
    bb, tm, d = x3.shape
    h = _rms(x3.reshape(bb * tm, d), g).reshape(bb, tm, d) * (1.0 + scale) + shift
    return h.reshape(bb * tm, d).astype(BF16)


def _gated_residual(x3, y2, g, gate):
    bb, tm, d = x3.shape
    return x3 + gate * _rms(y2, g).reshape(bb, tm, d)


def _resident_spec(shape):
    nd = len(shape)
    return pl.BlockSpec(shape, lambda *_: (0,) * nd, pipeline_mode=pl.Buffered(1))


_GRID2 = pltpu.CompilerParams(dimension_semantics=("arbitrary", "arbitrary"), vmem_limit_bytes=VMEM_LIMIT_BYTES)


def _mod_kernel(c_ref, w_ref, b_ref, o_ref):
    s = _silu(c_ref[...]).astype(BF16)
    o_ref[0] = _dot(s, w_ref[0].astype(BF16)) + b_ref[0]


def _modulation(c_all, w_mod, b_mod):
    depth, d, n = w_mod.shape
    return pl.pallas_call(
        _mod_kernel,
        grid=(depth, n // MOD_COL_TILE),
        in_specs=[
            pl.BlockSpec((MOD_ROWS, d), lambda i, j: (0, 0)),
            pl.BlockSpec((1, d, MOD_COL_TILE), lambda i, j: (i, 0, j)),
            pl.BlockSpec((1, 1, MOD_COL_TILE), lambda i, j: (i, 0, j)),
        ],
        out_specs=pl.BlockSpec((1, MOD_ROWS, MOD_COL_TILE), lambda i, j: (i, 0, j)),
        out_shape=jax.ShapeDtypeStruct((depth, MOD_ROWS, n), F32),
        compiler_params=_GRID2,
        name="modulation",
    )(c_all, w_mod, b_mod.reshape(depth, 1, n))


def _chunk_cumsum(g, bcum_scr):
    m, lanes = g.shape
    x = g.reshape(m // SUBLANES, SUBLANES, lanes)
    sub = lax.broadcasted_iota(jnp.int32, (1, SUBLANES, lanes), 1)
    s = 1
    while s < SUBLANES:
        x = x + jnp.where(sub >= s, pltpu.roll(x, s, axis=1), 0.0)
        s *= 2
    groups = CHUNK // SUBLANES
    totals = []
    for ci in range(m // CHUNK):
        carry = None
        for gi in range(groups):
            idx = ci * groups + gi
            blk = x[idx] if carry is None else x[idx] + carry
            bcum_scr[idx * SUBLANES:(idx + 1) * SUBLANES, :] = blk
            carry = jnp.broadcast_to(blk[SUBLANES - 1:SUBLANES, :], (SUBLANES, lanes))
        totals.append(carry)
    return totals


def _gla_kernel(x_ref, mod_ref, s0_ref, npre_ref, win_ref, wga_ref, wgb_ref, bg_ref, gn_ref, wout_ref,
                npost_ref, xo_ref, st_ref, proj_scr, bcum_scr, qd_scr, ki_scr, ke_scr, dexp_scr, o_scr, u_scr, sc_scr,
                *, bb, tm):
    kdim = wgb_ref.shape[-1]
    dk = kdim // GLA_HEADS
    vdim = wout_ref.shape[0]
    dv = vdim // GLA_HEADS
    rows_total = bb * tm
    n_chunks = tm // CHUNK
    chunks = [slice(ci * CHUNK, (ci + 1) * CHUNK) for ci in range(rows_total // CHUNK)]
    heads = [(slice(h * dk, (h + 1) * dk), slice(h * dv, (h + 1) * dv)) for h in range(GLA_HEADS)]

    @pl.when(pl.program_id(1) == 0)
    def _():
        st_ref[...] = s0_ref[...]
        dexp_scr[...] = jnp.zeros(dexp_scr.shape, F32)

    x3 = x_ref[...]
    m = mod_ref[...]
    hb = _modulated_norm(x3, npre_ref[...], m[:, 0:1, :], m[:, 1:2, :])
    proj_scr[...] = _dot(hb, win_ref[...])
    ga = _dot(hb, wga_ref[...]).astype(BF16)
    gl = _dot(ga, wgb_ref[...]) + bg_ref[...]
    totals = _chunk_cumsum(_log_sigmoid(gl) / GATE_TAU, bcum_scr)

    q_scale = dk ** -0.5
    for ci, rows in enumerate(chunks):
        bcum = bcum_scr[rows, :]
        blast = totals[ci][0:1, :]
        q = proj_scr[rows, 0:kdim]
        k = proj_scr[rows, kdim:2 * kdim]
        qd_scr[rows, :] = ((q * q_scale) * jnp.exp(bcum)).astype(BF16)
        ki_scr[rows, :] = (k * jnp.exp(-bcum)).astype(BF16)
        ke_scr[rows, :] = k * jnp.exp(blast - bcum)
        dexp_scr[ci:ci + 1, :] = jnp.exp(blast)
    decay_cols = dexp_scr[...].T

    row = lax.broadcasted_iota(jnp.int32, (CHUNK, CHUNK), 0)
    col = lax.broadcasted_iota(jnp.int32, (CHUNK, CHUNK), 1)
    causal = col <= row
    for ci, rows in enumerate(chunks):
        for h, (ks, vs) in enumerate(heads):
            scores = lax.dot_general(qd_scr[rows, ks], ki_scr[rows, ks], (((1,), (1,)), ((), ())),
                                     preferred_element_type=F32)
            sc_scr[ci, h] = jnp.where(causal, scores, 0.0).astype(BF16)
    for ci, rows in enumerate(chunks):
        for h, (ks, vs) in enumerate(heads):
            vh = proj_scr[rows, 2 * kdim + h * dv:2 * kdim + (h + 1) * dv].astype(BF16)
            keh_t = ke_scr[rows, ks].T.astype(BF16)
            both = _dot(jnp.concatenate([sc_scr[ci, h], keh_t], axis=0), vh)
            o_scr[rows, vs] = both[:CHUNK]
            u_scr[ci, h] = both[CHUNK:]

    gn = gn_ref[...]
    for ci, rows in enumerate(chunks):
        b = ci // n_chunks
        for h, (ks, vs) in enumerate(heads):
            s = st_ref[b, h]
            o = o_scr[rows, vs] + _dot(qd_scr[rows, ks], s.astype(BF16))
            o_scr[rows, vs] = _rms(o, gn)
            st_ref[b, h] = decay_cols[ks, ci:ci + 1] * s + u_scr[ci, h]

    r = proj_scr[:, 2 * kdim + vdim:]
    z = (o_scr[...] * _silu(r)).astype(BF16)
    y = _dot(z, wout_ref[...])
    xo_ref[...] = _gated_residual(x3, y, npost_ref[...], m[:, 2:3, :])


def _gla_layer(x, mod, s0, npre, w_in, w_ga, w_gb, b_g, gnorm, w_out, npost, *, bb, tm):
    bsz, t, d = x.shape
    _, heads, dk, dv = s0.shape
    pdim = w_in.shape[1]
    kdim, vdim = heads * dk, heads * dv
    rows = bb * tm
    chunks_pad = -(-(rows // CHUNK) // SUBLANES) * SUBLANES
    xspec = pl.BlockSpec((bb, tm, d), lambda i, j: (i, j, 0))
    sspec = pl.BlockSpec((bb, heads, dk, dv), lambda i, j: (i, 0, 0, 0))
    return pl.pallas_call(
        functools.partial(_gla_kernel, bb=bb, tm=tm),
        grid=(bsz // bb, t // tm),
        in_specs=[
            xspec,
            pl.BlockSpec((bb, N_MOD, d), lambda i, j: (i, 0, 0)),
            sspec,
            _resident_spec((1, d)),
            _resident_spec((d, pdim)),
            _resident_spec(w_ga.shape),
            _resident_spec(w_gb.shape),
            _resident_spec((1, kdim)),
            _resident_spec((1, dv)),
            _resident_spec((vdim, d)),
            _resident_spec((1, d)),
        ],
        out_specs=[xspec, sspec],
        out_shape=[jax.ShapeDtypeStruct(x.shape, F32), jax.ShapeDtypeStruct(s0.shape, F32)],
        scratch_shapes=[
            pltpu.VMEM((rows, pdim), F32),
            pltpu.VMEM((rows, kdim), F32),
            pltpu.VMEM((rows, kdim), BF16),
            pltpu.VMEM((rows, kdim), BF16),
            pltpu.VMEM((rows, kdim), F32),
            pltpu.VMEM((chunks_pad, kdim), F32),
            pltpu.VMEM((rows, vdim), F32),
            pltpu.VMEM((rows // CHUNK, heads, dk, dv), F32),
            pltpu.VMEM((rows // CHUNK, heads, CHUNK, CHUNK), BF16),
        ],
        compiler_params=_GRID2,
        name="gla_mixer",
    )(x, mod, s0, npre.reshape(1, d), w_in, w_ga, w_gb, b_g.reshape(1, kdim), gnorm.reshape(1, dv), w_out,
      npost.reshape(1, d))


def _conv_kernel(x_ref, mod_ref, cache_ref, npre_ref, win_ref, bin_ref, wdw_ref, bdw_ref, lng_ref, lnb_ref,
                 wout_ref, bout_ref, npost_ref, xo_ref, cache_out_ref, ext_scr, shift_scr, wtap_scr, y_scr,
                 *, bb, tm):
    c = wdw_ref.shape[-1]
    ext_rows = HIST_ROWS + tm

    @pl.when(pl.program_id(1) == 0)
    def _():
        ext_scr[:, 0:HIST_ROWS, :] = cache_ref[...]
        ext_scr[:, ext_rows:, :] = jnp.zeros((bb, SUBLANES, c), F32)
        for j in range(CONV_WIDTH):
            wtap_scr[j * SUBLANES:(j + 1) * SUBLANES, :] = jnp.broadcast_to(wdw_ref[j:j + 1, :], (SUBLANES, c))

    x3 = x_ref[...]
    m = mod_ref[...]
    hb = _modulated_norm(x3, npre_ref[...], m[:, 0:1, :], m[:, 1:2, :])
    u = _dot(hb, win_ref[...]) + bin_ref[...]
    glu = u[:, :c] * _sigmoid(u[:, c:])
    ext_scr[:, HIST_ROWS:ext_rows, :] = glu.reshape(bb, tm, c)

    n_shift_blocks = ext_rows // SHIFT_ROW_BLOCK

    def shift_step(idx, carry):
        b = idx // n_shift_blocks
        r0 = pl.multiple_of((idx % n_shift_blocks) * SHIFT_ROW_BLOCK, SHIFT_ROW_BLOCK)
        window = ext_scr[b, pl.ds(r0, SHIFT_ROW_BLOCK + SUBLANES), :]
        for s in range(1, SUBLANES):
            shift_scr[s - 1, b, pl.ds(r0, SHIFT_ROW_BLOCK), :] = (
                pltpu.roll(window, SHIFT_ROW_BLOCK + SUBLANES - s, axis=0)[0:SHIFT_ROW_BLOCK])
        return carry

    lax.fori_loop(0, bb * n_shift_blocks, shift_step, 0)

    n_row_blocks = tm // CONV_ROW_BLOCK

    def tap_step(idx, carry):
        b = idx // n_row_blocks
        r0 = pl.multiple_of((idx % n_row_blocks) * CONV_ROW_BLOCK, CONV_ROW_BLOCK)
        groups = CONV_ROW_BLOCK // SUBLANES
        for l0 in range(0, c, CONV_LANE_BLOCK):
            lanes = slice(l0, l0 + CONV_LANE_BLOCK)
            acc = jnp.zeros((CONV_ROW_BLOCK, CONV_LANE_BLOCK), F32)
            for s in range(SUBLANES):
                taps = [j for j in range(CONV_WIDTH) if (HIST_PAD + j) % SUBLANES == s]
                first = (HIST_PAD + taps[0]) // SUBLANES * SUBLANES
                span = (HIST_PAD + taps[-1]) // SUBLANES * SUBLANES - first + CONV_ROW_BLOCK
                rows = pl.ds(r0 + first, span)
                window = ext_scr[b, rows, lanes] if s == 0 else shift_scr[s - 1, b, rows, lanes]
                for j in taps:
                    a = (HIST_PAD + j) // SUBLANES * SUBLANES - first
                    w8 = wtap_scr[j * SUBLANES:(j + 1) * SUBLANES, lanes]
                    src = window[a:a + CONV_ROW_BLOCK].reshape(groups, SUBLANES, CONV_LANE_BLOCK)
                    acc = acc + (w8[None] * src).reshape(CONV_ROW_BLOCK, CONV_LANE_BLOCK)
            y_scr[b, pl.ds(r0, CONV_ROW_BLOCK), lanes] = acc
        return carry

    lax.fori_loop(0, bb * n_row_blocks, tap_step, 0)

    tail = ext_scr[:, tm:ext_rows, :]
    cache_out_ref[...] = tail
    ext_scr[:, 0:HIST_ROWS, :] = tail

    z = _layer_norm_silu(y_scr[...].reshape(bb * tm, c) + bdw_ref[...], lng_ref[...], lnb_ref[...]).astype(BF16)
    out = _dot(z, wout_ref[...]) + bout_ref[...]
    xo_ref[...] = _gated_residual(x3, out, npost_ref[...], m[:, 2:3, :])


def _conv_layer(x, mod, cache, npre, w_in, b_in, w_dw, b_dw, ln_g, ln_b, w_out, b_out, npost, *, bb, tm):
    bsz, t, d = x.shape
    c = w_dw.shape[-1]
    assert tm % CONV_ROW_BLOCK == 0 and tm >= HIST_ROWS
    xspec = pl.BlockSpec((bb, tm, d), lambda i, j: (i, j, 0))
    cspec = pl.BlockSpec((bb, HIST_ROWS, c), lambda i, j: (i, 0, 0))
    cache32 = jnp.pad(cache, ((0, 0), (HIST_PAD, 0), (0, 0)))
    x_new, cache_new = pl.pallas_call(
        functools.partial(_conv_kernel, bb=bb, tm=tm),
        grid=(bsz // bb, t // tm),
        in_specs=[
            xspec,
            pl.BlockSpec((bb, N_MOD, d), lambda i, j: (i, 0, 0)),
            cspec,
            _resident_spec((1, d)),
            _resident_spec(w_in.shape),
            _resident_spec((1, 2 * c)),
            _resident_spec(w_dw.shape),
            _resident_spec((1, c)),
            _resident_spec((1, c)),
            _resident_spec((1, c)),
            _resident_spec(w_out.shape),
            _resident_spec((1, d)),
            _resident_spec((1, d)),
        ],
        out_specs=[xspec, cspec],
        out_shape=[jax.ShapeDtypeStruct(x.shape, F32), jax.ShapeDtypeStruct((bsz, HIST_ROWS, c), F32)],
        scratch_shapes=[
            pltpu.VMEM((bb, HIST_ROWS + tm + SUBLANES, c), F32),
            pltpu.VMEM((SUBLANES - 1, bb, HIST_ROWS + tm, c), F32),
            pltpu.VMEM((CONV_WIDTH * SUBLANES, c), F32),
            pltpu.VMEM((bb, tm, c), F32),
        ],
        compiler_params=_GRID2,
        name="conv_mixer",
    )(x, mod, cache32, npre.reshape(1, d), w_in, b_in.reshape(1, 2 * c), w_dw, b_dw.reshape(1, c),
      ln_g.reshape(1, c), ln_b.reshape(1, c), w_out, b_out.reshape(1, d), npost.reshape(1, d))
    return x_new, cache_new[:, HIST_PAD:, :]


def _ffn_kernel(x_ref, mod_ref, npre_ref, wup_ref, wdn_ref, npost_ref, xo_ref):
    x3 = x_ref[...]
    bb, tm, d = x3.shape
    m = mod_ref[...]
    hb = _modulated_norm(x3, npre_ref[...], m[:, 3:4, :], m[:, 4:5, :])
    y = jnp.zeros((bb * tm, d), F32)
    for f0 in range(0, wup_ref.shape[1], FF_TILE):
        up = _dot(hb, wup_ref[:, f0:f0 + FF_TILE])
        y = y + _dot(jnp.square(jnp.maximum(up, 0.0)).astype(BF16), wdn_ref[f0:f0 + FF_TILE, :])
    xo_ref[...] = _gated_residual(x3, y, npost_ref[...], m[:, 5:6, :])


def _ffn_layer(x, mod, npre, w_up, w_dn, npost, *, bb, tm):
    bsz, t, d = x.shape
    xspec = pl.BlockSpec((bb, tm, d), lambda i, j: (i, j, 0))
    return pl.pallas_call(
        _ffn_kernel,
        grid=(bsz // bb, t // tm),
        in_specs=[
            xspec,
            pl.BlockSpec((bb, N_MOD, d), lambda i, j: (i, 0, 0)),
            _resident_spec((1, d)),
            _resident_spec(w_up.shape),
            _resident_spec(w_dn.shape),
            _resident_spec((1, d)),
        ],
        out_specs=xspec,
        out_shape=jax.ShapeDtypeStruct(x.shape, F32),
        compiler_params=_GRID2,
        name="ffn",
    )(x, mod, npre.reshape(1, d), w_up, w_dn, npost.reshape(1, d))


def _tiling(bsz, t):
    tm = min(t, ROW_TILE)
    bb = max(1, min(bsz, ROW_TILE // tm))
    assert t % tm == 0 and bsz % bb == 0 and tm % CHUNK == 0
    return bb, tm


def _run_trunk(x, mod, gla_states, conv_caches, p):
    bb, tm = _tiling(x.shape[0], x.shape[1])
    depth = p['w_ffn_up'].shape[0]
    new_gla, new_conv = [], []
    for i in range(depth):
        j = i // 2
        if i % 2 == 0:
            x, s = _gla_layer(x, mod[i], gla_states[j], p['norm_mix_pre'][i], p['gla_w_in'][j], p['gla_w_gate_a'][j],
                              p['gla_w_gate_b'][j], p['gla_b_gate'][j], p['gla_norm'][j], p['gla_w_out'][j],
                              p['norm_mix_post'][i], bb=bb, tm=tm)
            new_gla.append(s)
        else:
            x, s = _conv_layer(x, mod[i], conv_caches[j], p['norm_mix_pre'][i], p['conv_w_in'][j], p['conv_b_in'][j],
                               p['conv_w_dw'][j], p['conv_b_dw'][j], p['conv_ln_g'][j], p['conv_ln_b'][j],
                               p['conv_w_out'][j], p['conv_b_out'][j], p['norm_mix_post'][i], bb=bb, tm=tm)
            new_conv.append(s)
        x = _ffn_layer(x, mod[i], p['norm_ffn_pre'][i], p['w_ffn_up'][i], p['w_ffn_down'][i], p['norm_ffn_post'][i],
                       bb=bb, tm=tm)
    return x, jnp.stack(new_gla), jnp.stack(new_conv)


def kernel(x_prompt, x_sample, c_prompt, c_sample, state_gla, cache_conv, w_mod, b_mod, norm_mix_pre, norm_mix_post, norm_ffn_pre, norm_ffn_post, w_ffn_up, w_ffn_down, gla_w_in, gla_w_gate_a, gla_w_gate_b, gla_b_gate, gla_norm, gla_w_out, conv_w_in, conv_b_in, conv_w_dw, conv_b_dw, conv_ln_g, conv_ln_b, conv_w_out, conv_b_out):
    p = {
        'norm_mix_pre': norm_mix_pre, 'norm_mix_post': norm_mix_post,
        'norm_ffn_pre': norm_ffn_pre, 'norm_ffn_post': norm_ffn_post,
        'w_ffn_up': w_ffn_up.astype(BF16), 'w_ffn_down': w_ffn_down.astype(BF16),
        'gla_w_in': gla_w_in.astype(BF16), 'gla_w_gate_a': gla_w_gate_a.astype(BF16),
        'gla_w_gate_b': gla_w_gate_b.astype(BF16), 'gla_b_gate': gla_b_gate, 'gla_norm': gla_norm,
        'gla_w_out': gla_w_out.astype(BF16),
        'conv_w_in': conv_w_in.astype(BF16), 'conv_b_in': conv_b_in, 'conv_w_dw': conv_w_dw, 'conv_b_dw': conv_b_dw,
        'conv_ln_g': conv_ln_g, 'conv_ln_b': conv_ln_b, 'conv_w_out': conv_w_out.astype(BF16),
        'conv_b_out': conv_b_out,
    }
    depth, d, _ = w_mod.shape
    b_p, b_s = x_prompt.shape[0], x_sample.shape[0]
    n_gla, _, heads, dk, dv = state_gla.shape
    n_conv, _, hist, cdim = cache_conv.shape

    c_all = jnp.concatenate([c_prompt, c_sample, jnp.zeros((MOD_ROWS - b_p - b_s, d), F32)], axis=0)
    mod = _modulation(c_all, w_mod, b_mod)
    mod_p = mod[:, :b_p].reshape(depth, b_p, N_MOD, d)
    mod_s = mod[:, b_p:b_p + b_s].reshape(depth, b_s, N_MOD, d)

    gla0 = jnp.zeros((n_gla, b_p, heads, dk, dv), F32)
    conv0 = jnp.zeros((n_conv, b_p, hist, cdim), F32)
    y_p, gla_p, conv_p = _run_trunk(x_prompt, mod_p, gla0, conv0, p)
    y_s, gla_s, conv_s = _run_trunk(x_sample, mod_s, state_gla, cache_conv, p)
    return (y_p, y_s, gla_p, conv_p, gla_s, conv_s)
```

```python
import functools

import jax
import jax.numpy as jnp
from jax import lax
from jax.experimental import pallas as pl
from jax.experimental.pallas import tpu as pltpu

F32 = jnp.float32
BF16 = jnp.bfloat16

CHUNK = 64
GLA_HEADS = 4
GATE_TAU = 16.0
CONV_WIDTH = 31
N_MOD = 6
EPS = 1e-6

SUBLANES = 8
HIST_ROWS = 32
HIST_PAD = HIST_ROWS - (CONV_WIDTH - 1)
CONV_ROW_BLOCK = 64
CONV_LANE_BLOCK = 128
SHIFT_ROW_BLOCK = 32
FF_TILE = 1024
ROW_TILE = 512
VMEM_LIMIT_BYTES = 56 * 1024 * 1024
MOD_ROWS = 16
MOD_COL_TILE = 1536


def _sigmoid(x):
    return 1.0 / (1.0 + jnp.exp(-x))


def _silu(x):
    return x * _sigmoid(x)


def _log_sigmoid(x):
    return jnp.minimum(x, 0.0) - jnp.log(1.0 + jnp.exp(-jnp.abs(x)))


def _rms(x, g):
    return x * lax.rsqrt(jnp.mean(x * x, axis=-1, keepdims=True) + EPS) * g


def _layer_norm_silu(y, g, b):
    mu = jnp.mean(y, axis=-1, keepdims=True)
    yc = y - mu
    var = jnp.mean(yc * yc, axis=-1, keepdims=True)
    return _silu(yc * lax.rsqrt(var + EPS) * g + b)


def _dot(a, b):
    return jnp.dot(a, b, preferred_element_type=F32)


def _modulated_norm(x3, g, shift, scale):
    bb, tm, d = x3.shape
    h = _rms(x3.reshape(bb * tm, d), g).reshape(bb, tm, d) * (1.0 + scale) + shift
    return h.reshape(bb * tm, d).astype(BF16)


def _gated_residual(x3, y2, g, gate):
    bb, tm, d = x3.shape
    return x3 + gate * _rms(y2, g).reshape(bb, tm, d)


def _resident_spec(shape):
    nd = len(shape)
    return pl.BlockSpec(shape, lambda *_: (0,) * nd, pipeline_mode=pl.Buffered(1))


_GRID2 = pltpu.CompilerParams(dimension_semantics=("arbitrary", "arbitrary"), vmem_limit_bytes=VMEM_LIMIT_BYTES)


def _mod_kernel(c_ref, w_ref, b_ref, o_ref):
    s = _silu(c_ref[...]).astype(BF16)
    o_ref[0] = _dot(s, w_ref[0].astype(BF16)) + b_ref[0]


def _modulation(c_all, w_mod, b_mod):
    depth, d, n = w_mod.shape
    return pl.pallas_call(
        _mod_kernel,
        grid=(depth, n // MOD_COL_TILE),
        in_specs=[
            pl.BlockSpec((MOD_ROWS, d), lambda i, j: (0, 0)),
            pl.BlockSpec((1, d, MOD_COL_TILE), lambda i, j: (i, 0, j)),
            pl.BlockSpec((1, 1, MOD_COL_TILE), lambda i, j: (i, 0, j)),
        ],
        out_specs=pl.BlockSpec((1, MOD_ROWS, MOD_COL_TILE), lambda i, j: (i, 0, j)),
        out_shape=jax.ShapeDtypeStruct((depth, MOD_ROWS, n), F32),
        compiler_params=_GRID2,
        name="modulation",
    )(c_all, w_mod, b_mod.reshape(depth, 1, n))


def _chunk_cumsum(g, bcum_scr):
    m, lanes = g.shape
    x = g.reshape(m // SUBLANES, SUBLANES, lanes)
    sub = lax.broadcasted_iota(jnp.int32, (1, SUBLANES, lanes), 1)
    s = 1
    while s < SUBLANES:
        x = x + jnp.where(sub >= s, pltpu.roll(x, s, axis=1), 0.0)
        s *= 2
    groups = CHUNK // SUBLANES
    totals = []
    for ci in range(m // CHUNK):
        carry = None
        for gi in range(groups):
            idx = ci * groups + gi
            blk = x[idx] if carry is None else x[idx] + carry
            bcum_scr[idx * SUBLANES:(idx + 1) * SUBLANES, :] = blk
            carry = jnp.broadcast_to(blk[SUBLANES - 1:SUBLANES, :], (SUBLANES, lanes))
        totals.append(carry)
    return totals


def _gla_kernel(x_ref, mod_ref, s0_ref, npre_ref, win_ref, wga_ref, wgb_ref, bg_ref, gn_ref, wout_ref,
                npost_ref, xo_ref, st_ref, proj_scr, bcum_scr, qd_scr, ki_scr, ke_scr, dexp_scr, o_scr, u_scr, sc_scr,
                *, bb, tm):
    kdim = wgb_ref.shape[-1]
    dk = kdim // GLA_HEADS
    vdim = wout_ref.shape[0]
    dv = vdim // GLA_HEADS
    rows_total = bb * tm
    n_chunks = tm // CHUNK
    chunks = [slice(ci * CHUNK, (ci + 1) * CHUNK) for ci in range(rows_total // CHUNK)]
    heads = [(slice(h * dk, (h + 1) * dk), slice(h * dv, (h + 1) * dv)) for h in range(GLA_HEADS)]

    @pl.when(pl.program_id(1) == 0)
    def _():
        st_ref[...] = s0_ref[...]
        dexp_scr[...] = jnp.zeros(dexp_scr.shape, F32)

    x3 = x_ref[...]
    m = mod_ref[...]
    hb = _modulated_norm(x3, npre_ref[...], m[:, 0:1, :], m[:, 1:2, :])
    ga = _dot(hb, wga_ref[...]).astype(BF16)
    gl = _dot(ga, wgb_ref[...]) + bg_ref[...]
    totals = _chunk_cumsum(_log_sigmoid(gl) / GATE_TAU, bcum_scr)
    proj_scr[...] = _dot(hb, win_ref[...].astype(BF16))

    q_scale = dk ** -0.5
    for ci, rows in enumerate(chunks):
        bcum = bcum_scr[rows, :]
        blast = totals[ci][0:1, :]
        q = proj_scr[rows, 0:kdim]
        k = proj_scr[rows, kdim:2 * kdim]
        qd_scr[rows, :] = ((q * q_scale) * jnp.exp(bcum)).astype(BF16)
        ki_scr[rows, :] = (k * jnp.exp(-bcum)).astype(BF16)
        ke_scr[rows, :] = k * jnp.exp(blast - bcum)
        dexp_scr[ci:ci + 1, :] = jnp.exp(blast)
    decay_cols = dexp_scr[...].T

    row = lax.broadcasted_iota(jnp.int32, (CHUNK, CHUNK), 0)
    col = lax.broadcasted_iota(jnp.int32, (CHUNK, CHUNK), 1)
    causal = col <= row
    for ci, rows in enumerate(chunks):
        for h, (ks, vs) in enumerate(heads):
            scores = lax.dot_general(qd_scr[rows, ks], ki_scr[rows, ks], (((1,), (1,)), ((), ())),
                                     preferred_element_type=F32)
            sc_scr[ci, h] = jnp.where(causal, scores, 0.0).astype(BF16)
    for ci, rows in enumerate(chunks):
        for h, (ks, vs) in enumerate(heads):
            vh = proj_scr[rows, 2 * kdim + h * dv:2 * kdim + (h + 1) * dv].astype(BF16)
            keh_t = ke_scr[rows, ks].T.astype(BF16)
            both = _dot(jnp.concatenate([sc_scr[ci, h], keh_t], axis=0), vh)
            o_scr[rows, vs] = both[:CHUNK]
            u_scr[ci, h] = both[CHUNK:]

    gn = gn_ref[...]
    for ci, rows in enumerate(chunks):
        b = ci // n_chunks
        for h, (ks, vs) in enumerate(heads):
            s = st_ref[b, h]
            o = o_scr[rows, vs] + _dot(qd_scr[rows, ks], s.astype(BF16))
            o_scr[rows, vs] = _rms(o, gn)
            st_ref[b, h] = decay_cols[ks, ci:ci + 1] * s + u_scr[ci, h]

    r = proj_scr[:, 2 * kdim + vdim:]
    z = (o_scr[...] * _silu(r)).astype(BF16)
    y = _dot(z, wout_ref[...].astype(BF16))
    xo_ref[...] = _gated_residual(x3, y, npost_ref[...], m[:, 2:3, :])


def _gla_layer(x, mod, s0, npre, w_in, w_ga, w_gb, b_g, gnorm, w_out, npost, *, bb, tm):
    bsz, t, d = x.shape
    _, heads, dk, dv = s0.shape
    pdim = w_in.shape[1]
    kdim, vdim = heads * dk, heads * dv
    rows = bb * tm
    chunks_pad = -(-(rows // CHUNK) // SUBLANES) * SUBLANES
    xspec = pl.BlockSpec((bb, tm, d), lambda i, j: (i, j, 0))
    sspec = pl.BlockSpec((bb, heads, dk, dv), lambda i, j: (i, 0, 0, 0))
    return pl.pallas_call(
        functools.partial(_gla_kernel, bb=bb, tm=tm),
        grid=(bsz // bb, t // tm),
        in_specs=[
            xspec,
            pl.BlockSpec((bb, N_MOD, d), lambda i, j: (i, 0, 0)),
            sspec,
            _resident_spec((1, d)),
            _resident_spec((d, pdim)),
            _resident_spec(w_ga.shape),
            _resident_spec(w_gb.shape),
            _resident_spec((1, kdim)),
            _resident_spec((1, dv)),
            _resident_spec((vdim, d)),
            _resident_spec((1, d)),
        ],
        out_specs=[xspec, sspec],
        out_shape=[jax.ShapeDtypeStruct(x.shape, F32), jax.ShapeDtypeStruct(s0.shape, F32)],
        scratch_shapes=[
            pltpu.VMEM((rows, pdim), F32),
            pltpu.VMEM((rows, kdim), F32),
            pltpu.VMEM((rows, kdim), BF16),
            pltpu.VMEM((rows, kdim), BF16),
            pltpu.VMEM((rows, kdim), F32),
            pltpu.VMEM((chunks_pad, kdim), F32),
            pltpu.VMEM((rows, vdim), F32),
            pltpu.VMEM((rows // CHUNK, heads, dk, dv), F32),
            pltpu.VMEM((rows // CHUNK, heads, CHUNK, CHUNK), BF16),
        ],
        compiler_params=_GRID2,
        name="gla_mixer",
    )(x, mod, s0, npre.reshape(1, d), w_in, w_ga, w_gb, b_g.reshape(1, kdim), gnorm.reshape(1, dv), w_out,
      npost.reshape(1, d))


def _conv_kernel(x_ref, mod_ref, cache_ref, npre_ref, win_ref, bin_ref, wdw_ref, bdw_ref, lng_ref, lnb_ref,
                 wout_ref, bout_ref, npost_ref, xo_ref, cache_out_ref, ext_scr, shift_scr, wtap_scr, y_scr,
                 *, bb, tm):
    c = wdw_ref.shape[-1]
    ext_rows = HIST_ROWS + tm

    @pl.when(pl.program_id(1) == 0)
    def _():
        ext_scr[:, 0:HIST_ROWS, :] = cache_ref[...]
        ext_scr[:, ext_rows:, :] = jnp.zeros((bb, SUBLANES, c), F32)
        for j in range(CONV_WIDTH):
            wtap_scr[j * SUBLANES:(j + 1) * SUBLANES, :] = jnp.broadcast_to(wdw_ref[j:j + 1, :], (SUBLANES, c))

    x3 = x_ref[...]
    m = mod_ref[...]
    hb = _modulated_norm(x3, npre_ref[...], m[:, 0:1, :], m[:, 1:2, :])
    u = _dot(hb, win_ref[...].astype(BF16)) + bin_ref[...]
    glu = u[:, :c] * _sigmoid(u[:, c:])
    ext_scr[:, HIST_ROWS:ext_rows, :] = glu.reshape(bb, tm, c)

    n_shift_blocks = ext_rows // SHIFT_ROW_BLOCK

    def shift_step(idx, carry):
        b = idx // n_shift_blocks
        r0 = pl.multiple_of((idx % n_shift_blocks) * SHIFT_ROW_BLOCK, SHIFT_ROW_BLOCK)
        window = ext_scr[b, pl.ds(r0, SHIFT_ROW_BLOCK + SUBLANES), :]
        for s in range(1, SUBLANES):
            shift_scr[s - 1, b, pl.ds(r0, SHIFT_ROW_BLOCK), :] = (
                pltpu.roll(window, SHIFT_ROW_BLOCK + SUBLANES - s, axis=0)[0:SHIFT_ROW_BLOCK])
        return carry

    lax.fori_loop(0, bb * n_shift_blocks, shift_step, 0)

    n_row_blocks = tm // CONV_ROW_BLOCK

    def tap_step(idx, carry):
        b = idx // n_row_blocks
        r0 = pl.multiple_of((idx % n_row_blocks) * CONV_ROW_BLOCK, CONV_ROW_BLOCK)
        groups = CONV_ROW_BLOCK // SUBLANES
        for l0 in range(0, c, CONV_LANE_BLOCK):
            lanes = slice(l0, l0 + CONV_LANE_BLOCK)
            acc = jnp.zeros((CONV_ROW_BLOCK, CONV_LANE_BLOCK), F32)
            for s in range(SUBLANES):
                taps = [j for j in range(CONV_WIDTH) if (HIST_PAD + j) % SUBLANES == s]
                first = (HIST_PAD + taps[0]) // SUBLANES * SUBLANES
                span = (HIST_PAD + taps[-1]) // SUBLANES * SUBLANES - first + CONV_ROW_BLOCK
                rows = pl.ds(r0 + first, span)
                window = ext_scr[b, rows, lanes] if s == 0 else shift_scr[s - 1, b, rows, lanes]
                for j in taps:
                    a = (HIST_PAD + j) // SUBLANES * SUBLANES - first
                    w8 = wtap_scr[j * SUBLANES:(j + 1) * SUBLANES, lanes]
                    src = window[a:a + CONV_ROW_BLOCK].reshape(groups, SUBLANES, CONV_LANE_BLOCK)
                    acc = acc + (w8[None] * src).reshape(CONV_ROW_BLOCK, CONV_LANE_BLOCK)
            y_scr[b, pl.ds(r0, CONV_ROW_BLOCK), lanes] = acc
        return carry

    lax.fori_loop(0, bb * n_row_blocks, tap_step, 0)

    tail = ext_scr[:, tm:ext_rows, :]
    cache_out_ref[...] = tail
    ext_scr[:, 0:HIST_ROWS, :] = tail

    z = _layer_norm_silu(y_scr[...].reshape(bb * tm, c) + bdw_ref[...], lng_ref[...], lnb_ref[...]).astype(BF16)
    out = _dot(z, wout_ref[...].astype(BF16)) + bout_ref[...]
    xo_ref[...] = _gated_residual(x3, out, npost_ref[...], m[:, 2:3, :])


def _conv_layer(x, mod, cache, npre, w_in, b_in, w_dw, b_dw, ln_g, ln_b, w_out, b_out, npost, *, bb, tm):
    bsz, t, d = x.shape
    c = w_dw.shape[-1]
    assert tm % CONV_ROW_BLOCK == 0 and tm >= HIST_ROWS
    xspec = pl.BlockSpec((bb, tm, d), lambda i, j: (i, j, 0))
    cspec = pl.BlockSpec((bb, HIST_ROWS, c), lambda i, j: (i, 0, 0))
    cache32 = jnp.pad(cache, ((0, 0), (HIST_PAD, 0), (0, 0)))
    x_new, cache_new = pl.pallas_call(
        functools.partial(_conv_kernel, bb=bb, tm=tm),
        grid=(bsz // bb, t // tm),
        in_specs=[
            xspec,
            pl.BlockSpec((bb, N_MOD, d), lambda i, j: (i, 0, 0)),
            cspec,
            _resident_spec((1, d)),
            _resident_spec(w_in.shape),
            _resident_spec((1, 2 * c)),
            _resident_spec(w_dw.shape),
            _resident_spec((1, c)),
            _resident_spec((1, c)),
            _resident_spec((1, c)),
            _resident_spec(w_out.shape),
            _resident_spec((1, d)),
            _resident_spec((1, d)),
        ],
        out_specs=[xspec, cspec],
        out_shape=[jax.ShapeDtypeStruct(x.shape, F32), jax.ShapeDtypeStruct((bsz, HIST_ROWS, c), F32)],
        scratch_shapes=[
            pltpu.VMEM((bb, HIST_ROWS + tm + SUBLANES, c), F32),
            pltpu.VMEM((SUBLANES - 1, bb, HIST_ROWS + tm, c), F32),
            pltpu.VMEM((CONV_WIDTH * SUBLANES, c), F32),
            pltpu.VMEM((bb, tm, c), F32),
        ],
        compiler_params=_GRID2,
        name="conv_mixer",
    )(x, mod, cache32, npre.reshape(1, d), w_in, b_in.reshape(1, 2 * c), w_dw, b_dw.reshape(1, c),
      ln_g.reshape(1, c), ln_b.reshape(1, c), w_out, b_out.reshape(1, d), npost.reshape(1, d))
    return x_new, cache_new[:, HIST_PAD:, :]


def _ffn_kernel(x_ref, mod_ref, npre_ref, wup_ref, wdn_ref, npost_ref, xo_ref):
    x3 = x_ref[...]
    bb, tm, d = x3.shape
    m = mod_ref[...]
    hb = _modulated_norm(x3, npre_ref[...], m[:, 3:4, :], m[:, 4:5, :])
    y = jnp.zeros((bb * tm, d), F32)
    for f0 in range(0, wup_ref.shape[1], FF_TILE):
        up = _dot(hb, wup_ref[:, f0:f0 + FF_TILE].astype(BF16))
        y = y + _dot(jnp.square(jnp.maximum(up, 0.0)).astype(BF16), wdn_ref[f0:f0 + FF_TILE, :].astype(BF16))
    xo_ref[...] = _gated_residual(x3, y, npost_ref[...], m[:, 5:6, :])


def _ffn_layer(x, mod, npre, w_up, w_dn, npost, *, bb, tm):
    bsz, t, d = x.shape
    xspec = pl.BlockSpec((bb, tm, d), lambda i, j: (i, j, 0))
    return pl.pallas_call(
        _ffn_kernel,
        grid=(bsz // bb, t // tm),
        in_specs=[
            xspec,
            pl.BlockSpec((bb, N_MOD, d), lambda i, j: (i, 0, 0)),
            _resident_spec((1, d)),
            _resident_spec(w_up.shape),
            _resident_spec(w_dn.shape),
            _resident_spec((1, d)),
        ],
        out_specs=xspec,
        out_shape=jax.ShapeDtypeStruct(x.shape, F32),
        compiler_params=_GRID2,
        name="ffn",
    )(x, mod, npre.reshape(1, d), w_up, w_dn, npost.reshape(1, d))


def _tiling(bsz, t):
    tm = min(t, ROW_TILE)
    bb = max(1, min(bsz, ROW_TILE // tm))
    assert t % tm == 0 and bsz % bb == 0 and tm % CHUNK == 0
    return bb, tm


def _run_trunk(x, mod, gla_states, conv_caches, p):
    bb, tm = _tiling(x.shape[0], x.shape[1])
    depth = p['w_ffn_up'].shape[0]
    new_gla, new_conv = [], []
    for i in range(depth):
        j = i // 2
        if i % 2 == 0:
            x, s = _gla_layer(x, mod[i], gla_states[j], p['norm_mix_pre'][i], p['gla_w_in'][j], p['gla_w_gate_a'][j],
                              p['gla_w_gate_b'][j], p['gla_b_gate'][j], p['gla_norm'][j], p['gla_w_out'][j],
                              p['norm_mix_post'][i], bb=bb, tm=tm)
            new_gla.append(s)
        else:
            x, s = _conv_layer(x, mod[i], conv_caches[j], p['norm_mix_pre'][i], p['conv_w_in'][j], p['conv_b_in'][j],
                               p['conv_w_dw'][j], p['conv_b_dw'][j], p['conv_ln_g'][j], p['conv_ln_b'][j],
                               p['conv_w_out'][j], p['conv_b_out'][j], p['norm_mix_post'][i], bb=bb, tm=tm)
            new_conv.append(s)
        x = _ffn_layer(x, mod[i], p['norm_ffn_pre'][i], p['w_ffn_up'][i], p['w_ffn_down'][i], p['norm_ffn_post'][i],
                       bb=bb, tm=tm)
    return x, jnp.stack(new_gla), jnp.stack(new_conv)


def kernel(x_prompt, x_sample, c_prompt, c_sample, state_gla, cache_conv, w_mod, b_mod, norm_mix_pre, norm_mix_post, norm_ffn_pre, norm_ffn_post, w_ffn_up, w_ffn_down, gla_w_in, gla_w_gate_a, gla_w_gate_b, gla_b_gate, gla_norm, gla_w_out, conv_w_in, conv_b_in, conv_w_dw, conv_b_dw, conv_ln_g, conv_ln_b, conv_w_out, conv_b_out):
    p = {
        'norm_mix_pre': norm_mix_pre, 'norm_mix_post': norm_mix_post,
        'norm_ffn_pre': norm_ffn_pre, 'norm_ffn_post': norm_ffn_post,
        'w_ffn_up': w_ffn_up, 'w_ffn_down': w_ffn_down,
        'gla_w_in': gla_w_in, 'gla_w_gate_a': gla_w_gate_a.astype(BF16),
        'gla_w_gate_b': gla_w_gate_b.astype(BF16), 'gla_b_gate': gla_b_gate, 'gla_norm': gla_norm,
        'gla_w_out': gla_w_out,
        'conv_w_in': conv_w_in, 'conv_b_in': conv_b_in, 'conv_w_dw': conv_w_dw, 'conv_b_dw': conv_b_dw,
        'conv_ln_g': conv_ln_g, 'conv_ln_b': conv_ln_b, 'conv_w_out': conv_w_out,
        'conv_b_out': conv_b_out,
    }
    depth, d, _ = w_mod.shape
    b_p, b_s = x_prompt.shape[0], x_sample.shape[0]
    n_gla, _, heads, dk, dv = state_gla.shape
    n_conv, _, hist, cdim = cache_conv.shape

    c_all = jnp.concatenate([c_prompt, c_sample, jnp.zeros((MOD_ROWS - b_p - b_s, d), F32)], axis=0)
    mod = _modulation(c_all, w_mod, b_mod)
    mod_p = mod[:, :b_p].reshape(depth, b_p, N_MOD, d)
    mod_s = mod[:, b_p:b_p + b_s].reshape(depth, b_s, N_MOD, d)

    gla0 = jnp.zeros((n_gla, b_p, heads, dk, dv), F32)
    conv0 = jnp.zeros((n_conv, b_p, hist, cdim), F32)
    y_p, gla_p, conv_p = _run_trunk(x_prompt, mod_p, gla0, conv0, p)
    y_s, gla_s, conv_s = _run_trunk(x_sample, mod_s, state_gla, cache_conv, p)
    return (y_p, y_s, gla_p, conv_p, gla_s, conv_s)
```

```python
import functools

import jax
import jax.numpy as jnp
from jax import lax
from jax.experimental import pallas as pl
from jax.experimental.pallas import tpu as pltpu

F32 = jnp.float32
BF16 = jnp.bfloat16

CHUNK = 64
GLA_HEADS = 4
GATE_TAU = 16.0
CONV_WIDTH = 31
N_MOD = 6
EPS = 1e-6

SUBLANES = 8
HIST_ROWS = 32
HIST_PAD = HIST_ROWS - (CONV_WIDTH - 1)
CONV_ROW_BLOCK = 64
CONV_LANE_BLOCK = 128
SHIFT_ROW_BLOCK = 32
FF_TILE = 1024
ROW_TILE = 512
FFN_ROW_TILE = 1024
VMEM_LIMIT_BYTES = 56 * 1024 * 1024
MOD_ROWS = 16
MOD_COL_TILE = 1536


def _sigmoid(x):
    return 1.0 / (1.0 + jnp.exp(-x))


def _silu(x):
    return x * _sigmoid(x)


def _log_sigmoid(x):
    return jnp.minimum(x, 0.0) - jnp.log(1.0 + jnp.exp(-jnp.abs(x)))


def _rms(x, g):
    return x * lax.rsqrt(jnp.mean(x * x, axis=-1, keepdims=True) + EPS) * g


def _layer_norm_silu(y, g, b):
    mu = jnp.mean(y, axis=-1, keepdims=True)
    yc = y - mu
    var = jnp.mean(yc * yc, axis=-1, keepdims=True)
    return _silu(yc * lax.rsqrt(var + EPS) * g + b)


def _dot(a, b):
    return jnp.dot(a, b, preferred_element_type=F32)


def _modulated_norm(x3, g, shift, scale):
    bb, tm, d = x3.shape
    h = _rms(x3.reshape(bb * tm, d), g).reshape(bb, tm, d) * (1.0 + scale) + shift
    return h.reshape(bb * tm, d).astype(BF16)


def _gated_residual(x3, y2, g, gate):
    bb, tm, d = x3.shape
    return x3 + gate * _rms(y2, g).reshape(bb, tm, d)


def _resident_spec(shape):
    nd = len(shape)
    return pl.BlockSpec(shape, lambda *_: (0,) * nd, pipeline_mode=pl.Buffered(1))


_GRID2 = pltpu.CompilerParams(dimension_semantics=("arbitrary", "arbitrary"), vmem_limit_bytes=VMEM_LIMIT_BYTES)


def _mod_kernel(c_ref, w_ref, b_ref, o_ref):
    s = _silu(c_ref[...]).astype(BF16)
    o_ref[0] = _dot(s, w_ref[0].astype(BF16)) + b_ref[0]


def _modulation(c_all, w_mod, b_mod):
    depth, d, n = w_mod.shape
    return pl.pallas_call(
        _mod_kernel,
        grid=(depth, n // MOD_COL_TILE),
        in_specs=[
            pl.BlockSpec((MOD_ROWS, d), lambda i, j: (0, 0)),
            pl.BlockSpec((1, d, MOD_COL_TILE), lambda i, j: (i, 0, j)),
            pl.BlockSpec((1, 1, MOD_COL_TILE), lambda i, j: (i, 0, j)),
        ],
        out_specs=pl.BlockSpec((1, MOD_ROWS, MOD_COL_TILE), lambda i, j: (i, 0, j)),
        out_shape=jax.ShapeDtypeStruct((depth, MOD_ROWS, n), F32),
        compiler_params=_GRID2,
        name="modulation",
    )(c_all, w_mod, b_mod.reshape(depth, 1, n))


def _chunk_cumsum(g, bcum_scr):
    m, lanes = g.shape
    x = g.reshape(m // SUBLANES, SUBLANES, lanes)
    sub = lax.broadcasted_iota(jnp.int32, (1, SUBLANES, lanes), 1)
    s = 1
    while s < SUBLANES:
        x = x + jnp.where(sub >= s, pltpu.roll(x, s, axis=1), 0.0)
        s *= 2
    groups = CHUNK // SUBLANES
    totals = []
    for ci in range(m // CHUNK):
        carry = None
        for gi in range(groups):
            idx = ci * groups + gi
            blk = x[idx] if carry is None else x[idx] + carry
            bcum_scr[idx * SUBLANES:(idx + 1) * SUBLANES, :] = blk
            carry = jnp.broadcast_to(blk[SUBLANES - 1:SUBLANES, :], (SUBLANES, lanes))
        totals.append(carry)
    return totals


def _gla_kernel(x_ref, mod_ref, s0_ref, npre_ref, win_ref, wga_ref, wgb_ref, bg_ref, gn_ref, wout_ref,
                npost_ref, xo_ref, st_ref, proj_scr, bcum_scr, qd_scr, ki_scr, ke_scr, dexp_scr, o_scr, u_scr, sc_scr,
                *, bb, tm):
    kdim = wgb_ref.shape[-1]
    dk = kdim // GLA_HEADS
    vdim = wout_ref.shape[0]
    dv = vdim // GLA_HEADS
    rows_total = bb * tm
    n_chunks = tm // CHUNK
    chunks = [slice(ci * CHUNK, (ci + 1) * CHUNK) for ci in range(rows_total // CHUNK)]
    heads = [(slice(h * dk, (h + 1) * dk), slice(h * dv, (h + 1) * dv)) for h in range(GLA_HEADS)]

    @pl.when(pl.program_id(1) == 0)
    def _():
        st_ref[...] = s0_ref[...]
        dexp_scr[...] = jnp.zeros(dexp_scr.shape, F32)

    x3 = x_ref[...]
    m = mod_ref[...]
    hb = _modulated_norm(x3, npre_ref[...], m[:, 0:1, :], m[:, 1:2, :])
    ga = _dot(hb, wga_ref[...]).astype(BF16)
    gl = _dot(ga, wgb_ref[...]) + bg_ref[...]
    totals = _chunk_cumsum(_log_sigmoid(gl) / GATE_TAU, bcum_scr)
    proj_scr[...] = _dot(hb, win_ref[...].astype(BF16))

    q_scale = dk ** -0.5
    for ci, rows in enumerate(chunks):
        bcum = bcum_scr[rows, :]
        blast = totals[ci][0:1, :]
        q = proj_scr[rows, 0:kdim]
        k = proj_scr[rows, kdim:2 * kdim]
        qd_scr[rows, :] = ((q * q_scale) * jnp.exp(bcum)).astype(BF16)
        ki_scr[rows, :] = (k * jnp.exp(-bcum)).astype(BF16)
        ke_scr[rows, :] = k * jnp.exp(blast - bcum)
        dexp_scr[ci:ci + 1, :] = jnp.exp(blast)
    decay_cols = dexp_scr[...].T

    row = lax.broadcasted_iota(jnp.int32, (CHUNK, CHUNK), 0)
    col = lax.broadcasted_iota(jnp.int32, (CHUNK, CHUNK), 1)
    causal = col <= row
    for ci, rows in enumerate(chunks):
        for h, (ks, vs) in enumerate(heads):
            scores = lax.dot_general(qd_scr[rows, ks], ki_scr[rows, ks], (((1,), (1,)), ((), ())),
                                     preferred_element_type=F32)
            sc_scr[ci, h] = jnp.where(causal, scores, 0.0).astype(BF16)
    for ci, rows in enumerate(chunks):
        for h, (ks, vs) in enumerate(heads):
            vh = proj_scr[rows, 2 * kdim + h * dv:2 * kdim + (h + 1) * dv].astype(BF16)
            keh_t = ke_scr[rows, ks].T.astype(BF16)
            both = _dot(jnp.concatenate([sc_scr[ci, h], keh_t], axis=0), vh)
            o_scr[rows, vs] = both[:CHUNK]
            u_scr[ci, h] = both[CHUNK:]

    gn = gn_ref[...]
    for ci, rows in enumerate(chunks):
        b = ci // n_chunks
        for h, (ks, vs) in enumerate(heads):
            s = st_ref[b, h]
            o = o_scr[rows, vs] + _dot(qd_scr[rows, ks], s.astype(BF16))
            o_scr[rows, vs] = _rms(o, gn)
            st_ref[b, h] = decay_cols[ks, ci:ci + 1] * s + u_scr[ci, h]

    r = proj_scr[:, 2 * kdim + vdim:]
    z = (o_scr[...] * _silu(r)).astype(BF16)
    y = _dot(z, wout_ref[...].astype(BF16))
    xo_ref[...] = _gated_residual(x3, y, npost_ref[...], m[:, 2:3, :])


def _gla_layer(x, mod, s0, npre, w_in, w_ga, w_gb, b_g, gnorm, w_out, npost, *, bb, tm):
    bsz, t, d = x.shape
    _, heads, dk, dv = s0.shape
    pdim = w_in.shape[1]
    kdim, vdim = heads * dk, heads * dv
    rows = bb * tm
    chunks_pad = -(-(rows // CHUNK) // SUBLANES) * SUBLANES
    xspec = pl.BlockSpec((bb, tm, d), lambda i, j: (i, j, 0))
    sspec = pl.BlockSpec((bb, heads, dk, dv), lambda i, j: (i, 0, 0, 0))
    return pl.pallas_call(
        functools.partial(_gla_kernel, bb=bb, tm=tm),
        grid=(bsz // bb, t // tm),
        in_specs=[
            xspec,
            pl.BlockSpec((bb, N_MOD, d), lambda i, j: (i, 0, 0)),
            sspec,
            _resident_spec((1, d)),
            _resident_spec((d, pdim)),
            _resident_spec(w_ga.shape),
            _resident_spec(w_gb.shape),
            _resident_spec((1, kdim)),
            _resident_spec((1, dv)),
            _resident_spec((vdim, d)),
            _resident_spec((1, d)),
        ],
        out_specs=[xspec, sspec],
        out_shape=[jax.ShapeDtypeStruct(x.shape, F32), jax.ShapeDtypeStruct(s0.shape, F32)],
        scratch_shapes=[
            pltpu.VMEM((rows, pdim), F32),
            pltpu.VMEM((rows, kdim), F32),
            pltpu.VMEM((rows, kdim), BF16),
            pltpu.VMEM((rows, kdim), BF16),
            pltpu.VMEM((rows, kdim), F32),
            pltpu.VMEM((chunks_pad, kdim), F32),
            pltpu.VMEM((rows, vdim), F32),
            pltpu.VMEM((rows // CHUNK, heads, dk, dv), F32),
            pltpu.VMEM((rows // CHUNK, heads, CHUNK, CHUNK), BF16),
        ],
        compiler_params=_GRID2,
        name="gla_mixer",
    )(x, mod, s0, npre.reshape(1, d), w_in, w_ga, w_gb, b_g.reshape(1, kdim), gnorm.reshape(1, dv), w_out,
      npost.reshape(1, d))


def _conv_kernel(x_ref, mod_ref, cache_ref, npre_ref, win_ref, bin_ref, wdw_ref, bdw_ref, lng_ref, lnb_ref,
                 wout_ref, bout_ref, npost_ref, xo_ref, cache_out_ref, ext_scr, shift_scr, wtap_scr, y_scr,
                 *, bb, tm):
    c = wdw_ref.shape[-1]
    ext_rows = HIST_ROWS + tm

    @pl.when(pl.program_id(1) == 0)
    def _():
        ext_scr[:, 0:HIST_ROWS, :] = cache_ref[...]
        ext_scr[:, ext_rows:, :] = jnp.zeros((bb, SUBLANES, c), F32)
        for j in range(CONV_WIDTH):
            wtap_scr[j * SUBLANES:(j + 1) * SUBLANES, :] = jnp.broadcast_to(wdw_ref[j:j + 1, :], (SUBLANES, c))

    x3 = x_ref[...]
    m = mod_ref[...]
    hb = _modulated_norm(x3, npre_ref[...], m[:, 0:1, :], m[:, 1:2, :])
    u = _dot(hb, win_ref[...].astype(BF16)) + bin_ref[...]
    glu = u[:, :c] * _sigmoid(u[:, c:])
    ext_scr[:, HIST_ROWS:ext_rows, :] = glu.reshape(bb, tm, c)

    n_shift_blocks = ext_rows // SHIFT_ROW_BLOCK

    def shift_step(idx, carry):
        b = idx // n_shift_blocks
        r0 = pl.multiple_of((idx % n_shift_blocks) * SHIFT_ROW_BLOCK, SHIFT_ROW_BLOCK)
        window = ext_scr[b, pl.ds(r0, SHIFT_ROW_BLOCK + SUBLANES), :]
        for s in range(1, SUBLANES):
            shift_scr[s - 1, b, pl.ds(r0, SHIFT_ROW_BLOCK), :] = (
                pltpu.roll(window, SHIFT_ROW_BLOCK + SUBLANES - s, axis=0)[0:SHIFT_ROW_BLOCK])
        return carry

    lax.fori_loop(0, bb * n_shift_blocks, shift_step, 0)

    n_row_blocks = tm // CONV_ROW_BLOCK

    def tap_step(idx, carry):
        b = idx // n_row_blocks
        r0 = pl.multiple_of((idx % n_row_blocks) * CONV_ROW_BLOCK, CONV_ROW_BLOCK)
        groups = CONV_ROW_BLOCK // SUBLANES
        for l0 in range(0, c, CONV_LANE_BLOCK):
            lanes = slice(l0, l0 + CONV_LANE_BLOCK)
            acc = jnp.zeros((CONV_ROW_BLOCK, CONV_LANE_BLOCK), F32)
            for s in range(SUBLANES):
                taps = [j for j in range(CONV_WIDTH) if (HIST_PAD + j) % SUBLANES == s]
                first = (HIST_PAD + taps[0]) // SUBLANES * SUBLANES
                span = (HIST_PAD + taps[-1]) // SUBLANES * SUBLANES - first + CONV_ROW_BLOCK
                rows = pl.ds(r0 + first, span)
                window = ext_scr[b, rows, lanes] if s == 0 else shift_scr[s - 1, b, rows, lanes]
                for j in taps:
                    a = (HIST_PAD + j) // SUBLANES * SUBLANES - first
                    w8 = wtap_scr[j * SUBLANES:(j + 1) * SUBLANES, lanes]
                    src = window[a:a + CONV_ROW_BLOCK].reshape(groups, SUBLANES, CONV_LANE_BLOCK)
                    acc = acc + (w8[None] * src).reshape(CONV_ROW_BLOCK, CONV_LANE_BLOCK)
            y_scr[b, pl.ds(r0, CONV_ROW_BLOCK), lanes] = acc
        return carry

    lax.fori_loop(0, bb * n_row_blocks, tap_step, 0)

    tail = ext_scr[:, tm:ext_rows, :]
    cache_out_ref[...] = tail
    ext_scr[:, 0:HIST_ROWS, :] = tail

    z = _layer_norm_silu(y_scr[...].reshape(bb * tm, c) + bdw_ref[...], lng_ref[...], lnb_ref[...]).astype(BF16)
    out = _dot(z, wout_ref[...].astype(BF16)) + bout_ref[...]
    xo_ref[...] = _gated_residual(x3, out, npost_ref[...], m[:, 2:3, :])


def _conv_layer(x, mod, cache, npre, w_in, b_in, w_dw, b_dw, ln_g, ln_b, w_out, b_out, npost, *, bb, tm):
    bsz, t, d = x.shape
    c = w_dw.shape[-1]
    assert tm % CONV_ROW_BLOCK == 0 and tm >= HIST_ROWS
    xspec = pl.BlockSpec((bb, tm, d), lambda i, j: (i, j, 0))
    cspec = pl.BlockSpec((bb, HIST_ROWS, c), lambda i, j: (i, 0, 0))
    cache32 = jnp.pad(cache, ((0, 0), (HIST_PAD, 0), (0, 0)))
    x_new, cache_new = pl.pallas_call(
        functools.partial(_conv_kernel, bb=bb, tm=tm),
        grid=(bsz // bb, t // tm),
        in_specs=[
            xspec,
            pl.BlockSpec((bb, N_MOD, d), lambda i, j: (i, 0, 0)),
            cspec,
            _resident_spec((1, d)),
            _resident_spec(w_in.shape),
            _resident_spec((1, 2 * c)),
            _resident_spec(w_dw.shape),
            _resident_spec((1, c)),
            _resident_spec((1, c)),
            _resident_spec((1, c)),
            _resident_spec(w_out.shape),
            _resident_spec((1, d)),
            _resident_spec((1, d)),
        ],
        out_specs=[xspec, cspec],
        out_shape=[jax.ShapeDtypeStruct(x.shape, F32), jax.ShapeDtypeStruct((bsz, HIST_ROWS, c), F32)],
        scratch_shapes=[
            pltpu.VMEM((bb, HIST_ROWS + tm + SUBLANES, c), F32),
            pltpu.VMEM((SUBLANES - 1, bb, HIST_ROWS + tm, c), F32),
            pltpu.VMEM((CONV_WIDTH * SUBLANES, c), F32),
            pltpu.VMEM((bb, tm, c), F32),
        ],
        compiler_params=_GRID2,
        name="conv_mixer",
    )(x, mod, cache32, npre.reshape(1, d), w_in, b_in.reshape(1, 2 * c), w_dw, b_dw.reshape(1, c),
      ln_g.reshape(1, c), ln_b.reshape(1, c), w_out, b_out.reshape(1, d), npost.reshape(1, d))
    return x_new, cache_new[:, HIST_PAD:, :]


def _ffn_kernel(x_ref, mod_ref, npre_ref, wup_ref, wdn_ref, npost_ref, xo_ref):
    x3 = x_ref[...]
    bb, tm, d = x3.shape
    m = mod_ref[...]
    hb = _modulated_norm(x3, npre_ref[...], m[:, 3:4, :], m[:, 4:5, :])
    y = jnp.zeros((bb * tm, d), F32)
    for f0 in range(0, wup_ref.shape[1], FF_TILE):
        up = _dot(hb, wup_ref[:, f0:f0 + FF_TILE])
        y = y + _dot(jnp.square(jnp.maximum(up, 0.0)).astype(BF16), wdn_ref[f0:f0 + FF_TILE, :])
    xo_ref[...] = _gated_residual(x3, y, npost_ref[...], m[:, 5:6, :])


def _ffn_layer(x, mod, npre, w_up, w_dn, npost, *, bb, tm):
    bsz, t, d = x.shape
    xspec = pl.BlockSpec((bb, tm, d), lambda i, j: (i, j, 0))
    return pl.pallas_call(
        _ffn_kernel,
        grid=(bsz // bb, t // tm),
        in_specs=[
            xspec,
            pl.BlockSpec((bb, N_MOD, d), lambda i, j: (i, 0, 0)),
            _resident_spec((1, d)),
            _resident_spec(w_up.shape),
            _resident_spec(w_dn.shape),
            _resident_spec((1, d)),
        ],
        out_specs=xspec,
        out_shape=jax.ShapeDtypeStruct(x.shape, F32),
        compiler_params=_GRID2,
        name="ffn",
    )(x, mod, npre.reshape(1, d), w_up, w_dn, npost.reshape(1, d))


def _tiling(bsz, t, row_tile=ROW_TILE):
    tm = min(t, row_tile)
    bb = max(1, min(bsz, row_tile // tm))
    assert t % tm == 0 and bsz % bb == 0 and tm % CHUNK == 0
    return bb, tm


def _run_trunk(x, mod, gla_states, conv_caches, p):
    bb, tm = _tiling(x.shape[0], x.shape[1])
    depth = p['w_ffn_up'].shape[0]
    new_gla, new_conv = [], []
    for i in range(depth):
        j = i // 2
        if i % 2 == 0:
            x, s = _gla_layer(x, mod[i], gla_states[j], p['norm_mix_pre'][i], p['gla_w_in'][j], p['gla_w_gate_a'][j],
                              p['gla_w_gate_b'][j], p['gla_b_gate'][j], p['gla_norm'][j], p['gla_w_out'][j],
                              p['norm_mix_post'][i], bb=bb, tm=tm)
            new_gla.append(s)
        else:
            x, s = _conv_layer(x, mod[i], conv_caches[j], p['norm_mix_pre'][i], p['conv_w_in'][j], p['conv_b_in'][j],
                               p['conv_w_dw'][j], p['conv_b_dw'][j], p['conv_ln_g'][j], p['conv_ln_b'][j],
                               p['conv_w_out'][j], p['conv_b_out'][j], p['norm_mix_post'][i], bb=bb, tm=tm)
            new_conv.append(s)
        fb, ft = _tiling(x.shape[0], x.shape[1], FFN_ROW_TILE)
        x = _ffn_layer(x, mod[i], p['norm_ffn_pre'][i], p['w_ffn_up'][i], p['w_ffn_down'][i], p['norm_ffn_post'][i],
                       bb=fb, tm=ft)
    return x, jnp.stack(new_gla), jnp.stack(new_conv)


def kernel(x_prompt, x_sample, c_prompt, c_sample, state_gla, cache_conv, w_mod, b_mod, norm_mix_pre, norm_mix_post, norm_ffn_pre, norm_ffn_post, w_ffn_up, w_ffn_down, gla_w_in, gla_w_gate_a, gla_w_gate_b, gla_b_gate, gla_norm, gla_w_out, conv_w_in, conv_b_in, conv_w_dw, conv_b_dw, conv_ln_g, conv_ln_b, conv_w_out, conv_b_out):
    p = {
        'norm_mix_pre': norm_mix_pre, 'norm_mix_post': norm_mix_post,
        'norm_ffn_pre': norm_ffn_pre, 'norm_ffn_post': norm_ffn_post,
        'w_ffn_up': w_ffn_up.astype(BF16), 'w_ffn_down': w_ffn_down.astype(BF16),
        'gla_w_in': gla_w_in, 'gla_w_gate_a': gla_w_gate_a.astype(BF16),
        'gla_w_gate_b': gla_w_gate_b.astype(BF16), 'gla_b_gate': gla_b_gate, 'gla_norm': gla_norm,
        'gla_w_out': gla_w_out,
        'conv_w_in': conv_w_in, 'conv_b_in': conv_b_in, 'conv_w_dw': conv_w_dw, 'conv_b_dw': conv_b_dw,
        'conv_ln_g': conv_ln_g, 'conv_ln_b': conv_ln_b, 'conv_w_out': conv_w_out,
        'conv_b_out': conv_b_out,
    }
    depth, d, _ = w_mod.shape
    b_p, b_s = x_prompt.shape[0], x_sample.shape[0]
    n_gla, _, heads, dk, dv = state_gla.shape
    n_conv, _, hist, cdim = cache_conv.shape

    c_all = jnp.concatenate([c_prompt, c_sample, jnp.zeros((MOD_ROWS - b_p - b_s, d), F32)], axis=0)
    mod = _modulation(c_all, w_mod, b_mod)
    mod_p = mod[:, :b_p].reshape(depth, b_p, N_MOD, d)
    mod_s = mod[:, b_p:b_p + b_s].reshape(depth, b_s, N_MOD, d)

    gla0 = jnp.zeros((n_gla, b_p, heads, dk, dv), F32)
    conv0 = jnp.zeros((n_conv, b_p, hist, cdim), F32)
    y_p, gla_p, conv_p = _run_trunk(x_prompt, mod_p, gla0, conv0, p)
    y_s, gla_s, conv_s = _run_trunk(x_sample, mod_s, state_gla, cache_conv, p)
    return (y_p, y_s, gla_p, conv_p, gla_s, conv_s)
```

```python
import functools

import jax
import jax.numpy as jnp
from jax import lax
from jax.experimental import pallas as pl
from jax.experimental.pallas import tpu as pltpu

F32 = jnp.float32
BF16 = jnp.bfloat16

CHUNK = 64
GLA_HEADS = 4
GATE_TAU = 16.0
CONV_WIDTH = 31
N_MOD = 6
EPS = 1e-6

SUBLANES = 8
HIST_ROWS = 32
HIST_PAD = HIST_ROWS - (CONV_WIDTH - 1)
CONV_ROW_BLOCK = 64
CONV_LANE_BLOCK = 128
SHIFT_ROW_BLOCK = 32
FF_TILE = 1024
ROW_TILE = 512
VMEM_LIMIT_BYTES = 56 * 1024 * 1024
MOD_ROWS = 16
MOD_COL_TILE = 3072


def _sigmoid(x):
    return 1.0 / (1.0 + jnp.exp(-x))


def _silu(x):
    return x * _sigmoid(x)


def _log_sigmoid(x):
    return jnp.minimum(x, 0.0) - jnp.log(1.0 + jnp.exp(-jnp.abs(x)))


def _rms(x, g):
    return x * lax.rsqrt(jnp.mean(x * x, axis=-1, keepdims=True) + EPS) * g


def _layer_norm_silu(y, g, b):
    mu = jnp.mean(y, axis=-1, keepdims=True)
    yc = y - mu
    var = jnp.mean(yc * yc, axis=-1, keepdims=True)
    return _silu(yc * lax.rsqrt(var + EPS) * g + b)


def _dot(a, b):
    return jnp.dot(a, b, preferred_element_type=F32)


def _modulated_norm(x3, g, shift, scale):
    bb, tm, d = x3.shape
    h = _rms(x3.reshape(bb * tm, d), g).reshape(bb, tm, d) * (1.0 + scale) + shift
    return h.reshape(bb * tm, d).astype(BF16)


def _gated_residual(x3, y2, g, gate):
    bb, tm, d = x3.shape
    return x3 + gate * _rms(y2, g).reshape(bb, tm, d)


def _resident_spec(shape):
    nd = len(shape)
    return pl.BlockSpec(shape, lambda *_: (0,) * nd, pipeline_mode=pl.Buffered(1))


_GRID2 = pltpu.CompilerParams(dimension_semantics=("arbitrary", "arbitrary"), vmem_limit_bytes=VMEM_LIMIT_BYTES)


def _mod_kernel(c_ref, w_ref, b_ref, o_ref):
    s = _silu(c_ref[...]).astype(BF16)
    o_ref[0] = _dot(s, w_ref[0].astype(BF16)) + b_ref[0]


def _modulation(c_all, w_mod, b_mod):
    depth, d, n = w_mod.shape
    return pl.pallas_call(
        _mod_kernel,
        grid=(depth, n // MOD_COL_TILE),
        in_specs=[
            pl.BlockSpec((MOD_ROWS, d), lambda i, j: (0, 0)),
            pl.BlockSpec((1, d, MOD_COL_TILE), lambda i, j: (i, 0, j)),
            pl.BlockSpec((1, 1, MOD_COL_TILE), lambda i, j: (i, 0, j)),
        ],
        out_specs=pl.BlockSpec((1, MOD_ROWS, MOD_COL_TILE), lambda i, j: (i, 0, j)),
        out_shape=jax.ShapeDtypeStruct((depth, MOD_ROWS, n), F32),
        compiler_params=_GRID2,
        name="modulation",
    )(c_all, w_mod, b_mod.reshape(depth, 1, n))


def _chunk_cumsum(g, bcum_scr):
    m, lanes = g.shape
    x = g.reshape(m // SUBLANES, SUBLANES, lanes)
    sub = lax.broadcasted_iota(jnp.int32, (1, SUBLANES, lanes), 1)
    s = 1
    while s < SUBLANES:
        x = x + jnp.where(sub >= s, pltpu.roll(x, s, axis=1), 0.0)
        s *= 2
    groups = CHUNK // SUBLANES
    totals = []
    for ci in range(m // CHUNK):
        carry = None
        for gi in range(groups):
            idx = ci * groups + gi
            blk = x[idx] if carry is None else x[idx] + carry
            bcum_scr[idx * SUBLANES:(idx + 1) * SUBLANES, :] = blk
            carry = jnp.broadcast_to(blk[SUBLANES - 1:SUBLANES, :], (SUBLANES, lanes))
        totals.append(carry)
    return totals


def _gla_kernel(x_ref, mod_ref, s0_ref, npre_ref, win_ref, wga_ref, wgb_ref, bg_ref, gn_ref, wout_ref,
                npost_ref, xo_ref, st_ref, proj_scr, bcum_scr, qd_scr, ki_scr, ke_scr, dexp_scr, o_scr, u_scr, sc_scr,
                *, bb, tm):
    kdim = wgb_ref.shape[-1]
    dk = kdim // GLA_HEADS
    vdim = wout_ref.shape[0]
    dv = vdim // GLA_HEADS
    rows_total = bb * tm
    n_chunks = tm // CHUNK
    chunks = [slice(ci * CHUNK, (ci + 1) * CHUNK) for ci in range(rows_total // CHUNK)]
    heads = [(slice(h * dk, (h + 1) * dk), slice(h * dv, (h + 1) * dv)) for h in range(GLA_HEADS)]

    @pl.when(pl.program_id(1) == 0)
    def _():
        st_ref[...] = s0_ref[...]
        dexp_scr[...] = jnp.zeros(dexp_scr.shape, F32)

    x3 = x_ref[...]
    m = mod_ref[...]
    hb = _modulated_norm(x3, npre_ref[...], m[:, 0:1, :], m[:, 1:2, :])
    ga = _dot(hb, wga_ref[...]).astype(BF16)
    gl = _dot(ga, wgb_ref[...]) + bg_ref[...]
    totals = _chunk_cumsum(_log_sigmoid(gl) / GATE_TAU, bcum_scr)
    proj_scr[...] = _dot(hb, win_ref[...].astype(BF16))

    q_scale = dk ** -0.5
    for ci, rows in enumerate(chunks):
        bcum = bcum_scr[rows, :]
        blast = totals[ci][0:1, :]
        q = proj_scr[rows, 0:kdim]
        k = proj_scr[rows, kdim:2 * kdim]
        qd_scr[rows, :] = ((q * q_scale) * jnp.exp(bcum)).astype(BF16)
        ki_scr[rows, :] = (k * jnp.exp(-bcum)).astype(BF16)
        ke_scr[rows, :] = k * jnp.exp(blast - bcum)
        dexp_scr[ci:ci + 1, :] = jnp.exp(blast)
    decay_cols = dexp_scr[...].T

    row = lax.broadcasted_iota(jnp.int32, (CHUNK, CHUNK), 0)
    col = lax.broadcasted_iota(jnp.int32, (CHUNK, CHUNK), 1)
    causal = col <= row
    for ci, rows in enumerate(chunks):
        for h, (ks, vs) in enumerate(heads):
            scores = lax.dot_general(qd_scr[rows, ks], ki_scr[rows, ks], (((1,), (1,)), ((), ())),
                                     preferred_element_type=F32)
            sc_scr[ci, h] = jnp.where(causal, scores, 0.0).astype(BF16)
    for ci, rows in enumerate(chunks):
        for h, (ks, vs) in enumerate(heads):
            vh = proj_scr[rows, 2 * kdim + h * dv:2 * kdim + (h + 1) * dv].astype(BF16)
            keh_t = ke_scr[rows, ks].T.astype(BF16)
            both = _dot(jnp.concatenate([sc_scr[ci, h], keh_t], axis=0), vh)
            o_scr[rows, vs] = both[:CHUNK]
            u_scr[ci, h] = both[CHUNK:]

    gn = gn_ref[...]
    for ci, rows in enumerate(chunks):
        b = ci // n_chunks
        for h, (ks, vs) in enumerate(heads):
            s = st_ref[b, h]
            o = o_scr[rows, vs] + _dot(qd_scr[rows, ks], s.astype(BF16))
            o_scr[rows, vs] = _rms(o, gn)
            st_ref[b, h] = decay_cols[ks, ci:ci + 1] * s + u_scr[ci, h]

    r = proj_scr[:, 2 * kdim + vdim:]
    z = (o_scr[...] * _silu(r)).astype(BF16)
    y = _dot(z, wout_ref[...].astype(BF16))
    xo_ref[...] = _gated_residual(x3, y, npost_ref[...], m[:, 2:3, :])


def _gla_layer(x, mod, s0, npre, w_in, w_ga, w_gb, b_g, gnorm, w_out, npost, *, bb, tm):
    bsz, t, d = x.shape
    _, heads, dk, dv = s0.shape
    pdim = w_in.shape[1]
    kdim, vdim = heads * dk, heads * dv
    rows = bb * tm
    chunks_pad = -(-(rows // CHUNK) // SUBLANES) * SUBLANES
    xspec = pl.BlockSpec((bb, tm, d), lambda i, j: (i, j, 0))
    sspec = pl.BlockSpec((bb, heads, dk, dv), lambda i, j: (i, 0, 0, 0))
    return pl.pallas_call(
        functools.partial(_gla_kernel, bb=bb, tm=tm),
        grid=(bsz // bb, t // tm),
        in_specs=[
            xspec,
            pl.BlockSpec((bb, N_MOD, d), lambda i, j: (i, 0, 0)),
            sspec,
            _resident_spec((1, d)),
            _resident_spec((d, pdim)),
            _resident_spec(w_ga.shape),
            _resident_spec(w_gb.shape),
            _resident_spec((1, kdim)),
            _resident_spec((1, dv)),
            _resident_spec((vdim, d)),
            _resident_spec((1, d)),
        ],
        out_specs=[xspec, sspec],
        out_shape=[jax.ShapeDtypeStruct(x.shape, F32), jax.ShapeDtypeStruct(s0.shape, F32)],
        scratch_shapes=[
            pltpu.VMEM((rows, pdim), F32),
            pltpu.VMEM((rows, kdim), F32),
            pltpu.VMEM((rows, kdim), BF16),
            pltpu.VMEM((rows, kdim), BF16),
            pltpu.VMEM((rows, kdim), F32),
            pltpu.VMEM((chunks_pad, kdim), F32),
            pltpu.VMEM((rows, vdim), F32),
            pltpu.VMEM((rows // CHUNK, heads, dk, dv), F32),
            pltpu.VMEM((rows // CHUNK, heads, CHUNK, CHUNK), BF16),
        ],
        compiler_params=_GRID2,
        name="gla_mixer",
    )(x, mod, s0, npre.reshape(1, d), w_in, w_ga, w_gb, b_g.reshape(1, kdim), gnorm.reshape(1, dv), w_out,
      npost.reshape(1, d))


def _conv_kernel(x_ref, mod_ref, cache_ref, npre_ref, win_ref, bin_ref, wdw_ref, bdw_ref, lng_ref, lnb_ref,
                 wout_ref, bout_ref, npost_ref, xo_ref, cache_out_ref, ext_scr, shift_scr, wtap_scr, y_scr,
                 *, bb, tm):
    c = wdw_ref.shape[-1]
    ext_rows = HIST_ROWS + tm

    @pl.when(pl.program_id(1) == 0)
    def _():
        ext_scr[:, 0:HIST_ROWS, :] = cache_ref[...]
        ext_scr[:, ext_rows:, :] = jnp.zeros((bb, SUBLANES, c), F32)
        for j in range(CONV_WIDTH):
            wtap_scr[j * SUBLANES:(j + 1) * SUBLANES, :] = jnp.broadcast_to(wdw_ref[j:j + 1, :], (SUBLANES, c))

    x3 = x_ref[...]
    m = mod_ref[...]
    hb = _modulated_norm(x3, npre_ref[...], m[:, 0:1, :], m[:, 1:2, :])
    u = _dot(hb, win_ref[...].astype(BF16)) + bin_ref[...]
    glu = u[:, :c] * _sigmoid(u[:, c:])
    ext_scr[:, HIST_ROWS:ext_rows, :] = glu.reshape(bb, tm, c)

    n_shift_blocks = ext_rows // SHIFT_ROW_BLOCK

    def shift_step(idx, carry):
        b = idx // n_shift_blocks
        r0 = pl.multiple_of((idx % n_shift_blocks) * SHIFT_ROW_BLOCK, SHIFT_ROW_BLOCK)
        groups = SHIFT_ROW_BLOCK // SUBLANES
        window = ext_scr[b, pl.ds(r0, SHIFT_ROW_BLOCK + SUBLANES), :].reshape(groups + 1, SUBLANES, c)
        sub = lax.broadcasted_iota(jnp.int32, (1, SUBLANES, c), 1)
        for s in range(1, SUBLANES):
            mixed = jnp.where(sub >= s, window[:groups], window[1:])
            shifted = pltpu.roll(mixed, SUBLANES - s, axis=1)
            shift_scr[s - 1, b, pl.ds(r0, SHIFT_ROW_BLOCK), :] = shifted.reshape(SHIFT_ROW_BLOCK, c)
        return carry

    lax.fori_loop(0, bb * n_shift_blocks, shift_step, 0)

    n_row_blocks = tm // CONV_ROW_BLOCK

    def tap_step(idx, carry):
        b = idx // n_row_blocks
        r0 = pl.multiple_of((idx % n_row_blocks) * CONV_ROW_BLOCK, CONV_ROW_BLOCK)
        groups = CONV_ROW_BLOCK // SUBLANES
        for l0 in range(0, c, CONV_LANE_BLOCK):
            lanes = slice(l0, l0 + CONV_LANE_BLOCK)
            acc = jnp.zeros((CONV_ROW_BLOCK, CONV_LANE_BLOCK), F32)
            for s in range(SUBLANES):
                taps = [j for j in range(CONV_WIDTH) if (HIST_PAD + j) % SUBLANES == s]
                first = (HIST_PAD + taps[0]) // SUBLANES * SUBLANES
                span = (HIST_PAD + taps[-1]) // SUBLANES * SUBLANES - first + CONV_ROW_BLOCK
                rows = pl.ds(r0 + first, span)
                window = ext_scr[b, rows, lanes] if s == 0 else shift_scr[s - 1, b, rows, lanes]
                for j in taps:
                    a = (HIST_PAD + j) // SUBLANES * SUBLANES - first
                    w8 = wtap_scr[j * SUBLANES:(j + 1) * SUBLANES, lanes]
                    src = window[a:a + CONV_ROW_BLOCK].reshape(groups, SUBLANES, CONV_LANE_BLOCK)
                    acc = acc + (w8[None] * src).reshape(CONV_ROW_BLOCK, CONV_LANE_BLOCK)
            y_scr[b, pl.ds(r0, CONV_ROW_BLOCK), lanes] = acc
        return carry

    lax.fori_loop(0, bb * n_row_blocks, tap_step, 0)

    tail = ext_scr[:, tm:ext_rows, :]
    cache_out_ref[...] = tail
    ext_scr[:, 0:HIST_ROWS, :] = tail

    z = _layer_norm_silu(y_scr[...].reshape(bb * tm, c) + bdw_ref[...], lng_ref[...], lnb_ref[...]).astype(BF16)
    out = _dot(z, wout_ref[...].astype(BF16)) + bout_ref[...]
    xo_ref[...] = _gated_residual(x3, out, npost_ref[...], m[:, 2:3, :])


def _conv_layer(x, mod, cache, npre, w_in, b_in, w_dw, b_dw, ln_g, ln_b, w_out, b_out, npost, *, bb, tm):
    bsz, t, d = x.shape
    c = w_dw.shape[-1]
    assert tm % CONV_ROW_BLOCK == 0 and tm >= HIST_ROWS
    xspec = pl.BlockSpec((bb, tm, d), lambda i, j: (i, j, 0))
    cspec = pl.BlockSpec((bb, HIST_ROWS, c), lambda i, j: (i, 0, 0))
    cache32 = jnp.pad(cache, ((0, 0), (HIST_PAD, 0), (0, 0)))
    x_new, cache_new = pl.pallas_call(
        functools.partial(_conv_kernel, bb=bb, tm=tm),
        grid=(bsz // bb, t // tm),
        in_specs=[
            xspec,
            pl.BlockSpec((bb, N_MOD, d), lambda i, j: (i, 0, 0)),
            cspec,
            _resident_spec((1, d)),
            _resident_spec(w_in.shape),
            _resident_spec((1, 2 * c)),
            _resident_spec(w_dw.shape),
            _resident_spec((1, c)),
            _resident_spec((1, c)),
            _resident_spec((1, c)),
            _resident_spec(w_out.shape),
            _resident_spec((1, d)),
            _resident_spec((1, d)),
        ],
        out_specs=[xspec, cspec],
        out_shape=[jax.ShapeDtypeStruct(x.shape, F32), jax.ShapeDtypeStruct((bsz, HIST_ROWS, c), F32)],
        scratch_shapes=[
            pltpu.VMEM((bb, HIST_ROWS + tm + SUBLANES, c), F32),
            pltpu.VMEM((SUBLANES - 1, bb, HIST_ROWS + tm, c), F32),
            pltpu.VMEM((CONV_WIDTH * SUBLANES, c), F32),
            pltpu.VMEM((bb, tm, c), F32),
        ],
        compiler_params=_GRID2,
        name="conv_mixer",
    )(x, mod, cache32, npre.reshape(1, d), w_in, b_in.reshape(1, 2 * c), w_dw, b_dw.reshape(1, c),
      ln_g.reshape(1, c), ln_b.reshape(1, c), w_out, b_out.reshape(1, d), npost.reshape(1, d))
    return x_new, cache_new[:, HIST_PAD:, :]


def _ffn_kernel(x_ref, mod_ref, npre_ref, wup_ref, wdn_ref, npost_ref, xo_ref):
    x3 = x_ref[...]
    bb, tm, d = x3.shape
    m = mod_ref[...]
    hb = _modulated_norm(x3, npre_ref[...], m[:, 3:4, :], m[:, 4:5, :])
    y = jnp.zeros((bb * tm, d), F32)
    for f0 in range(0, wup_ref.shape[1], FF_TILE):
        up = _dot(hb, wup_ref[:, f0:f0 + FF_TILE])
        y = y + _dot(jnp.square(jnp.maximum(up, 0.0)).astype(BF16), wdn_ref[f0:f0 + FF_TILE, :])
    xo_ref[...] = _gated_residual(x3, y, npost_ref[...], m[:, 5:6, :])


def _ffn_layer(x, mod, npre, w_up, w_dn, npost, *, layer, bb, tm):
    bsz, t, d = x.shape

    def layer_spec(w):
        return pl.BlockSpec((None,) + w.shape[1:], lambda i, j: (layer, 0, 0), pipeline_mode=pl.Buffered(1))

    xspec = pl.BlockSpec((bb, tm, d), lambda i, j: (i, j, 0))
    return pl.pallas_call(
        _ffn_kernel,
        grid=(bsz // bb, t // tm),
        in_specs=[
            xspec,
            pl.BlockSpec((bb, N_MOD, d), lambda i, j: (i, 0, 0)),
            _resident_spec((1, d)),
            layer_spec(w_up),
            layer_spec(w_dn),
            _resident_spec((1, d)),
        ],
        out_specs=xspec,
        out_shape=jax.ShapeDtypeStruct(x.shape, F32),
        compiler_params=_GRID2,
        name="ffn",
    )(x, mod, npre.reshape(1, d), w_up, w_dn, npost.reshape(1, d))


def _tiling(bsz, t):
    tm = min(t, ROW_TILE)
    bb = max(1, min(bsz, ROW_TILE // tm))
    assert t % tm == 0 and bsz % bb == 0 and tm % CHUNK == 0
    return bb, tm


def _run_trunk(x, mod, gla_states, conv_caches, p):
    bb, tm = _tiling(x.shape[0], x.shape[1])
    depth = p['w_ffn_up'].shape[0]
    new_gla, new_conv = [], []
    for i in range(depth):
        j = i // 2
        if i % 2 == 0:
            x, s = _gla_layer(x, mod[i], gla_states[j], p['norm_mix_pre'][i], p['gla_w_in'][j], p['gla_w_gate_a'][j],
                              p['gla_w_gate_b'][j], p['gla_b_gate'][j], p['gla_norm'][j], p['gla_w_out'][j],
                              p['norm_mix_post'][i], bb=bb, tm=tm)
            new_gla.append(s)
        else:
            x, s = _conv_layer(x, mod[i], conv_caches[j], p['norm_mix_pre'][i], p['conv_w_in'][j], p['conv_b_in'][j],
                               p['conv_w_dw'][j], p['conv_b_dw'][j], p['conv_ln_g'][j], p['conv_ln_b'][j],
                               p['conv_w_out'][j], p['conv_b_out'][j], p['norm_mix_post'][i], bb=bb, tm=tm)
            new_conv.append(s)
        x = _ffn_layer(x, mod[i], p['norm_ffn_pre'][i], p['w_ffn_up'], p['w_ffn_down'], p['norm_ffn_post'][i],
                       layer=i, bb=bb, tm=tm)
    return x, jnp.stack(new_gla), jnp.stack(new_conv)


def kernel(x_prompt, x_sample, c_prompt, c_sample, state_gla, cache_conv, w_mod, b_mod, norm_mix_pre, norm_mix_post, norm_ffn_pre, norm_ffn_post, w_ffn_up, w_ffn_down, gla_w_in, gla_w_gate_a, gla_w_gate_b, gla_b_gate, gla_norm, gla_w_out, conv_w_in, conv_b_in, conv_w_dw, conv_b_dw, conv_ln_g, conv_ln_b, conv_w_out, conv_b_out):
    p = {
        'norm_mix_pre': norm_mix_pre, 'norm_mix_post': norm_mix_post,
        'norm_ffn_pre': norm_ffn_pre, 'norm_ffn_post': norm_ffn_post,
        'w_ffn_up': w_ffn_up.astype(BF16), 'w_ffn_down': w_ffn_down.astype(BF16),
        'gla_w_in': gla_w_in, 'gla_w_gate_a': gla_w_gate_a.astype(BF16),
        'gla_w_gate_b': gla_w_gate_b.astype(BF16), 'gla_b_gate': gla_b_gate, 'gla_norm': gla_norm,
        'gla_w_out': gla_w_out,
        'conv_w_in': conv_w_in, 'conv_b_in': conv_b_in, 'conv_w_dw': conv_w_dw, 'conv_b_dw': conv_b_dw,
        'conv_ln_g': conv_ln_g, 'conv_ln_b': conv_ln_b, 'conv_w_out': conv_w_out,
        'conv_b_out': conv_b_out,
    }
    depth, d, _ = w_mod.shape
    b_p, b_s = x_prompt.shape[0], x_sample.shape[0]
    n_gla, _, heads, dk, dv = state_gla.shape
    n_conv, _, hist, cdim = cache_conv.shape

    c_all = jnp.concatenate([c_prompt, c_sample, jnp.zeros((MOD_ROWS - b_p - b_s, d), F32)], axis=0)
    mod = _modulation(c_all, w_mod, b_mod)
    mod_p = mod[:, :b_p].reshape(depth, b_p, N_MOD, d)
    mod_s = mod[:, b_p:b_p + b_s].reshape(depth, b_s, N_MOD, d)

    gla0 = jnp.zeros((n_gla, b_p, heads, dk, dv), F32)
    conv0 = jnp.zeros((n_conv, b_p, hist, cdim), F32)
    y_p, gla_p, conv_p = _run_trunk(x_prompt, mod_p, gla0, conv0, p)
    y_s, gla_s, conv_s = _run_trunk(x_sample, mod_s, state_gla, cache_conv, p)
    return (y_p, y_s, gla_p, conv_p, gla_s, conv_s)
```

```python
import functools

import jax
import jax.numpy as jnp
from jax import lax
from jax.experimental import pallas as pl
from jax.experimental.pallas import tpu as pltpu

F32 = jnp.float32
BF16 = jnp.bfloat16

CHUNK = 64
GLA_HEADS = 4
GATE_TAU = 16.0
CONV_WIDTH = 31
N_MOD = 6
EPS = 1e-6

SUBLANES = 8
HIST_ROWS = 32
HIST_PAD = HIST_ROWS - (CONV_WIDTH - 1)
CONV_ROW_BLOCK = 64
CONV_LANE_BLOCK = 128
SHIFT_ROW_BLOCK = 32
FF_TILE = 1024
ROW_TILE = 512
VMEM_LIMIT_BYTES = 56 * 1024 * 1024
MOD_ROWS = 16
MOD_COL_TILE = 1536


def _sigmoid(x):
    return 1.0 / (1.0 + jnp.exp(-x))


def _silu(x):
    return x * _sigmoid(x)


def _log_sigmoid(x):
    return jnp.minimum(x, 0.0) - jnp.log(1.0 + jnp.exp(-jnp.abs(x)))


def _rms(x, g):
    return x * lax.rsqrt(jnp.mean(x * x, axis=-1, keepdims=True) + EPS) * g


def _layer_norm_silu(y, g, b):
    mu = jnp.mean(y, axis=-1, keepdims=True)
    yc = y - mu
    var = jnp.mean(yc * yc, axis=-1, keepdims=True)
    return _silu(yc * lax.rsqrt(var + EPS) * g + b)


def _dot(a, b):
    return jnp.dot(a, b, preferred_element_type=F32)


def _modulated_norm(x3, g, shift, scale):
    bb, tm, d = x3.shape
    h = _rms(x3.reshape(bb * tm, d), g).reshape(bb, tm, d) * (1.0 + scale) + shift
    return h.reshape(bb * tm, d).astype(BF16)


def _gated_residual(x3, y2, g, gate):
    bb, tm, d = x3.shape
    return x3 + gate * _rms(y2, g).reshape(bb, tm, d)


def _resident_spec(shape):
    nd = len(shape)
    return pl.BlockSpec(shape, lambda *_: (0,) * nd, pipeline_mode=pl.Buffered(1))


_GRID2 = pltpu.CompilerParams(dimension_semantics=("arbitrary", "arbitrary"), vmem_limit_bytes=VMEM_LIMIT_BYTES)


def _mod_kernel(c_ref, w_ref, b_ref, o_ref):
    s = _silu(c_ref[...]).astype(BF16)
    o_ref[0] = _dot(s, w_ref[0].astype(BF16)) + b_ref[0]


def _modulation(c_all, w_mod, b_mod):
    depth, d, n = w_mod.shape
    return pl.pallas_call(
        _mod_kernel,
        grid=(depth, n // MOD_COL_TILE),
        in_specs=[
            pl.BlockSpec((MOD_ROWS, d), lambda i, j: (0, 0)),
            pl.BlockSpec((1, d, MOD_COL_TILE), lambda i, j: (i, 0, j)),
            pl.BlockSpec((1, 1, MOD_COL_TILE), lambda i, j: (i, 0, j)),
        ],
        out_specs=pl.BlockSpec((1, MOD_ROWS, MOD_COL_TILE), lambda i, j: (i, 0, j)),
        out_shape=jax.ShapeDtypeStruct((depth, MOD_ROWS, n), F32),
        compiler_params=_GRID2,
        name="modulation",
    )(c_all, w_mod, b_mod.reshape(depth, 1, n))


def _chunk_cumsum(g, bcum_scr):
    m, lanes = g.shape
    x = g.reshape(m // SUBLANES, SUBLANES, lanes)
    sub = lax.broadcasted_iota(jnp.int32, (1, SUBLANES, lanes), 1)
    s = 1
    while s < SUBLANES:
        x = x + jnp.where(sub >= s, pltpu.roll(x, s, axis=1), 0.0)
        s *= 2
    groups = CHUNK // SUBLANES
    totals = []
    for ci in range(m // CHUNK):
        carry = None
        for gi in range(groups):
            idx = ci * groups + gi
            blk = x[idx] if carry is None else x[idx] + carry
            bcum_scr[idx * SUBLANES:(idx + 1) * SUBLANES, :] = blk
            carry = jnp.broadcast_to(blk[SUBLANES - 1:SUBLANES, :], (SUBLANES, lanes))
        totals.append(carry)
    return totals


def _gla_kernel(x_ref, mod_ref, s0_ref, npre_ref, win_ref, wga_ref, wgb_ref, bg_ref, gn_ref, wout_ref,
                npost_ref, xo_ref, st_ref, proj_scr, bcum_scr, qd_scr, ki_scr, ke_scr, dexp_scr, o_scr, u_scr, sc_scr,
                *, bb, tm):
    kdim = wgb_ref.shape[-1]
    dk = kdim // GLA_HEADS
    vdim = wout_ref.shape[0]
    dv = vdim // GLA_HEADS
    rows_total = bb * tm
    n_chunks = tm // CHUNK
    chunks = [slice(ci * CHUNK, (ci + 1) * CHUNK) for ci in range(rows_total // CHUNK)]
    heads = [(slice(h * dk, (h + 1) * dk), slice(h * dv, (h + 1) * dv)) for h in range(GLA_HEADS)]

    @pl.when(pl.program_id(1) == 0)
    def _():
        st_ref[...] = s0_ref[...]
        dexp_scr[...] = jnp.zeros(dexp_scr.shape, F32)

    x3 = x_ref[...]
    m = mod_ref[...]
    hb = _modulated_norm(x3, npre_ref[...], m[:, 0:1, :], m[:, 1:2, :])
    ga = _dot(hb, wga_ref[...]).astype(BF16)
    gl = _dot(ga, wgb_ref[...]) + bg_ref[...]
    totals = _chunk_cumsum(_log_sigmoid(gl) / GATE_TAU, bcum_scr)
    proj_scr[...] = _dot(hb, win_ref[...].astype(BF16))

    q_scale = dk ** -0.5
    for ci, rows in enumerate(chunks):
        bcum = bcum_scr[rows, :]
        blast = totals[ci][0:1, :]
        q = proj_scr[rows, 0:kdim]
        k = proj_scr[rows, kdim:2 * kdim]
        qd_scr[rows, :] = ((q * q_scale) * jnp.exp(bcum)).astype(BF16)
        ki_scr[rows, :] = (k * jnp.exp(-bcum)).astype(BF16)
        ke_scr[rows, :] = k * jnp.exp(blast - bcum)
        dexp_scr[ci:ci + 1, :] = jnp.exp(blast)
    decay_cols = dexp_scr[...].T

    row = lax.broadcasted_iota(jnp.int32, (CHUNK, CHUNK), 0)
    col = lax.broadcasted_iota(jnp.int32, (CHUNK, CHUNK), 1)
    causal = col <= row
    for ci, rows in enumerate(chunks):
        for h, (ks, vs) in enumerate(heads):
            scores = lax.dot_general(qd_scr[rows, ks], ki_scr[rows, ks], (((1,), (1,)), ((), ())),
                                     preferred_element_type=F32)
            sc_scr[ci, h] = jnp.where(causal, scores, 0.0).astype(BF16)
    for ci, rows in enumerate(chunks):
        for h, (ks, vs) in enumerate(heads):
            vh = proj_scr[rows, 2 * kdim + h * dv:2 * kdim + (h + 1) * dv].astype(BF16)
            keh_t = ke_scr[rows, ks].T.astype(BF16)
            both = _dot(jnp.concatenate([sc_scr[ci, h], keh_t], axis=0), vh)
            o_scr[rows, vs] = both[:CHUNK]
            u_scr[ci, h] = both[CHUNK:]

    gn = gn_ref[...]
    for ci, rows in enumerate(chunks):
        b = ci // n_chunks
        for h, (ks, vs) in enumerate(heads):
            s = st_ref[b, h]
            o = o_scr[rows, vs] + _dot(qd_scr[rows, ks], s.astype(BF16))
            o_scr[rows, vs] = _rms(o, gn)
            st_ref[b, h] = decay_cols[ks, ci:ci + 1] * s + u_scr[ci, h]

    r = proj_scr[:, 2 * kdim + vdim:]
    z = (o_scr[...] * _silu(r)).astype(BF16)
    y = _dot(z, wout_ref[...].astype(BF16))
    xo_ref[...] = _gated_residual(x3, y, npost_ref[...], m[:, 2:3, :])


def _gla_layer(x, mod, s0, npre, w_in, w_ga, w_gb, b_g, gnorm, w_out, npost, *, bb, tm):
    bsz, t, d = x.shape
    _, heads, dk, dv = s0.shape
    pdim = w_in.shape[1]
    kdim, vdim = heads * dk, heads * dv
    rows = bb * tm
    chunks_pad = -(-(rows // CHUNK) // SUBLANES) * SUBLANES
    xspec = pl.BlockSpec((bb, tm, d), lambda i, j: (i, j, 0))
    sspec = pl.BlockSpec((bb, heads, dk, dv), lambda i, j: (i, 0, 0, 0))
    return pl.pallas_call(
        functools.partial(_gla_kernel, bb=bb, tm=tm),
        grid=(bsz // bb, t // tm),
        in_specs=[
            xspec,
            pl.BlockSpec((bb, N_MOD, d), lambda i, j: (i, 0, 0)),
            sspec,
            _resident_spec((1, d)),
            _resident_spec((d, pdim)),
            _resident_spec(w_ga.shape),
            _resident_spec(w_gb.shape),
            _resident_spec((1, kdim)),
            _resident_spec((1, dv)),
            _resident_spec((vdim, d)),
            _resident_spec((1, d)),
        ],
        out_specs=[xspec, sspec],
        out_shape=[jax.ShapeDtypeStruct(x.shape, F32), jax.ShapeDtypeStruct(s0.shape, F32)],
        scratch_shapes=[
            pltpu.VMEM((rows, pdim), F32),
            pltpu.VMEM((rows, kdim), F32),
            pltpu.VMEM((rows, kdim), BF16),
            pltpu.VMEM((rows, kdim), BF16),
            pltpu.VMEM((rows, kdim), F32),
            pltpu.VMEM((chunks_pad, kdim), F32),
            pltpu.VMEM((rows, vdim), F32),
            pltpu.VMEM((rows // CHUNK, heads, dk, dv), F32),
            pltpu.VMEM((rows // CHUNK, heads, CHUNK, CHUNK), BF16),
        ],
        compiler_params=_GRID2,
        name="gla_mixer",
    )(x, mod, s0, npre.reshape(1, d), w_in, w_ga, w_gb, b_g.reshape(1, kdim), gnorm.reshape(1, dv), w_out,
      npost.reshape(1, d))


def _conv_kernel(x_ref, mod_ref, cache_ref, npre_ref, win_ref, bin_ref, wdw_ref, bdw_ref, lng_ref, lnb_ref,
                 wout_ref, bout_ref, npost_ref, xo_ref, cache_out_ref, ext_scr, shift_scr, wtap_scr, y_scr,
                 *, bb, tm):
    c = wdw_ref.shape[-1]
    ext_rows = HIST_ROWS + tm

    @pl.when(pl.program_id(1) == 0)
    def _():
        ext_scr[:, 0:HIST_ROWS, :] = cache_ref[...]
        ext_scr[:, ext_rows:, :] = jnp.zeros((bb, SUBLANES, c), F32)
        for j in range(CONV_WIDTH):
            wtap_scr[j * SUBLANES:(j + 1) * SUBLANES, :] = jnp.broadcast_to(wdw_ref[j:j + 1, :], (SUBLANES, c))

    x3 = x_ref[...]
    m = mod_ref[...]
    hb = _modulated_norm(x3, npre_ref[...], m[:, 0:1, :], m[:, 1:2, :])
    u = _dot(hb, win_ref[...].astype(BF16)) + bin_ref[...]
    glu = u[:, :c] * _sigmoid(u[:, c:])
    ext_scr[:, HIST_ROWS:ext_rows, :] = glu.reshape(bb, tm, c)

    n_shift_blocks = ext_rows // SHIFT_ROW_BLOCK

    def shift_step(idx, carry):
        b = idx // n_shift_blocks
        r0 = pl.multiple_of((idx % n_shift_blocks) * SHIFT_ROW_BLOCK, SHIFT_ROW_BLOCK)
        groups = SHIFT_ROW_BLOCK // SUBLANES
        window = ext_scr[b, pl.ds(r0, SHIFT_ROW_BLOCK + SUBLANES), :].reshape(groups + 1, SUBLANES, c)
        sub = lax.broadcasted_iota(jnp.int32, (1, SUBLANES, c), 1)
        for s in range(1, SUBLANES):
            mixed = jnp.where(sub >= s, window[:groups], window[1:])
            shifted = pltpu.roll(mixed, SUBLANES - s, axis=1)
            shift_scr[s - 1, b, pl.ds(r0, SHIFT_ROW_BLOCK), :] = shifted.reshape(SHIFT_ROW_BLOCK, c)
        return carry

    lax.fori_loop(0, bb * n_shift_blocks, shift_step, 0)

    n_row_blocks = tm // CONV_ROW_BLOCK

    def tap_step(idx, carry):
        b = idx // n_row_blocks
        r0 = pl.multiple_of((idx % n_row_blocks) * CONV_ROW_BLOCK, CONV_ROW_BLOCK)
        groups = CONV_ROW_BLOCK // SUBLANES
        for l0 in range(0, c, CONV_LANE_BLOCK):
            lanes = slice(l0, l0 + CONV_LANE_BLOCK)
            acc = jnp.zeros((CONV_ROW_BLOCK, CONV_LANE_BLOCK), F32)
            for s in range(SUBLANES):
                taps = [j for j in range(CONV_WIDTH) if (HIST_PAD + j) % SUBLANES == s]
                first = (HIST_PAD + taps[0]) // SUBLANES * SUBLANES
                span = (HIST_PAD + taps[-1]) // SUBLANES * SUBLANES - first + CONV_ROW_BLOCK
                rows = pl.ds(r0 + first, span)
                window = ext_scr[b, rows, lanes] if s == 0 else shift_scr[s - 1, b, rows, lanes]
                for j in taps:
                    a = (HIST_PAD + j) // SUBLANES * SUBLANES - first
                    w8 = wtap_scr[j * SUBLANES:(j + 1) * SUBLANES, lanes]
                    src = window[a:a + CONV_ROW_BLOCK].reshape(groups, SUBLANES, CONV_LANE_BLOCK)
                    acc = acc + (w8[None] * src).reshape(CONV_ROW_BLOCK, CONV_LANE_BLOCK)
            y_scr[b, pl.ds(r0, CONV_ROW_BLOCK), lanes] = acc
        return carry

    lax.fori_loop(0, bb * n_row_blocks, tap_step, 0)

    tail = ext_scr[:, tm:ext_rows, :]
    cache_out_ref[...] = tail
    ext_scr[:, 0:HIST_ROWS, :] = tail

    z = _layer_norm_silu(y_scr[...].reshape(bb * tm, c) + bdw_ref[...], lng_ref[...], lnb_ref[...]).astype(BF16)
    out = _dot(z, wout_ref[...].astype(BF16)) + bout_ref[...]
    xo_ref[...] = _gated_residual(x3, out, npost_ref[...], m[:, 2:3, :])


def _conv_layer(x, mod, cache, npre, w_in, b_in, w_dw, b_dw, ln_g, ln_b, w_out, b_out, npost, *, bb, tm):
    bsz, t, d = x.shape
    c = w_dw.shape[-1]
    assert tm % CONV_ROW_BLOCK == 0 and tm >= HIST_ROWS
    xspec = pl.BlockSpec((bb, tm, d), lambda i, j: (i, j, 0))
    cspec = pl.BlockSpec((bb, HIST_ROWS, c), lambda i, j: (i, 0, 0))
    cache32 = jnp.pad(cache, ((0, 0), (HIST_PAD, 0), (0, 0)))
    x_new, cache_new = pl.pallas_call(
        functools.partial(_conv_kernel, bb=bb, tm=tm),
        grid=(bsz // bb, t // tm),
        in_specs=[
            xspec,
            pl.BlockSpec((bb, N_MOD, d), lambda i, j: (i, 0, 0)),
            cspec,
            _resident_spec((1, d)),
            _resident_spec(w_in.shape),
            _resident_spec((1, 2 * c)),
            _resident_spec(w_dw.shape),
            _resident_spec((1, c)),
            _resident_spec((1, c)),
            _resident_spec((1, c)),
            _resident_spec(w_out.shape),
            _resident_spec((1, d)),
            _resident_spec((1, d)),
        ],
        out_specs=[xspec, cspec],
        out_shape=[jax.ShapeDtypeStruct(x.shape, F32), jax.ShapeDtypeStruct((bsz, HIST_ROWS, c), F32)],
        scratch_shapes=[
            pltpu.VMEM((bb, HIST_ROWS + tm + SUBLANES, c), F32),
            pltpu.VMEM((SUBLANES - 1, bb, HIST_ROWS + tm, c), F32),
            pltpu.VMEM((CONV_WIDTH * SUBLANES, c), F32),
            pltpu.VMEM((bb, tm, c), F32),
        ],
        compiler_params=_GRID2,
        name="conv_mixer",
    )(x, mod, cache32, npre.reshape(1, d), w_in, b_in.reshape(1, 2 * c), w_dw, b_dw.reshape(1, c),
      ln_g.reshape(1, c), ln_b.reshape(1, c), w_out, b_out.reshape(1, d), npost.reshape(1, d))
    return x_new, cache_new[:, HIST_PAD:, :]


def _ffn_kernel(x_ref, mod_ref, npre_ref, wup_ref, wdn_ref, npost_ref, xo_ref):
    x3 = x_ref[...]
    bb, tm, d = x3.shape
    m = mod_ref[...]
    hb = _modulated_norm(x3, npre_ref[...], m[:, 3:4, :], m[:, 4:5, :])
    y = jnp.zeros((bb * tm, d), F32)
    for f0 in range(0, wup_ref.shape[1], FF_TILE):
        up = _dot(hb, wup_ref[:, f0:f0 + FF_TILE])
        y = y + _dot(jnp.square(jnp.maximum(up, 0.0)).astype(BF16), wdn_ref[f0:f0 + FF_TILE, :])
    xo_ref[...] = _gated_residual(x3, y, npost_ref[...], m[:, 5:6, :])


def _ffn_layer(x, mod, npre, w_up, w_dn, npost, *, bb, tm):
    bsz, t, d = x.shape
    xspec = pl.BlockSpec((bb, tm, d), lambda i, j: (i, j, 0))
    return pl.pallas_call(
        _ffn_kernel,
        grid=(bsz // bb, t // tm),
        in_specs=[
            xspec,
            pl.BlockSpec((bb, N_MOD, d), lambda i, j: (i, 0, 0)),
            _resident_spec((1, d)),
            _resident_spec(w_up.shape),
            _resident_spec(w_dn.shape),
            _resident_spec((1, d)),
        ],
        out_specs=xspec,
        out_shape=jax.ShapeDtypeStruct(x.shape, F32),
        compiler_params=_GRID2,
        name="ffn",
    )(x, mod, npre.reshape(1, d), w_up, w_dn, npost.reshape(1, d))


def _tiling(bsz, t):
    tm = min(t, ROW_TILE)
    bb = max(1, min(bsz, ROW_TILE // tm))
    assert t % tm == 0 and bsz % bb == 0 and tm % CHUNK == 0
    return bb, tm


def _run_trunk(x, mod, gla_states, conv_caches, p):
    bb, tm = _tiling(x.shape[0], x.shape[1])
    depth = len(p['w_ffn_up'])
    new_gla, new_conv = [], []
    for i in range(depth):
        j = i // 2
        if i % 2 == 0:
            x, s = _gla_layer(x, mod[i], gla_states[j], p['norm_mix_pre'][i], p['gla_w_in'][j], p['gla_w_gate_a'][j],
                              p['gla_w_gate_b'][j], p['gla_b_gate'][j], p['gla_norm'][j], p['gla_w_out'][j],
                              p['norm_mix_post'][i], bb=bb, tm=tm)
            new_gla.append(s)
        else:
            x, s = _conv_layer(x, mod[i], conv_caches[j], p['norm_mix_pre'][i], p['conv_w_in'][j], p['conv_b_in'][j],
                               p['conv_w_dw'][j], p['conv_b_dw'][j], p['conv_ln_g'][j], p['conv_ln_b'][j],
                               p['conv_w_out'][j], p['conv_b_out'][j], p['norm_mix_post'][i], bb=bb, tm=tm)
            new_conv.append(s)
        x = _ffn_layer(x, mod[i], p['norm_ffn_pre'][i], p['w_ffn_up'][i], p['w_ffn_down'][i], p['norm_ffn_post'][i],
                       bb=bb, tm=tm)
    return x, jnp.stack(new_gla), jnp.stack(new_conv)


def kernel(x_prompt, x_sample, c_prompt, c_sample, state_gla, cache_conv, w_mod, b_mod, norm_mix_pre, norm_mix_post, norm_ffn_pre, norm_ffn_post, w_ffn_up, w_ffn_down, gla_w_in, gla_w_gate_a, gla_w_gate_b, gla_b_gate, gla_norm, gla_w_out, conv_w_in, conv_b_in, conv_w_dw, conv_b_dw, conv_ln_g, conv_ln_b, conv_w_out, conv_b_out):
    p = {
        'norm_mix_pre': norm_mix_pre, 'norm_mix_post': norm_mix_post,
        'norm_ffn_pre': norm_ffn_pre, 'norm_ffn_post': norm_ffn_post,
        'w_ffn_up': [w.astype(BF16) for w in w_ffn_up], 'w_ffn_down': [w.astype(BF16) for w in w_ffn_down],
        'gla_w_in': gla_w_in, 'gla_w_gate_a': gla_w_gate_a.astype(BF16),
        'gla_w_gate_b': gla_w_gate_b.astype(BF16), 'gla_b_gate': gla_b_gate, 'gla_norm': gla_norm,
        'gla_w_out': gla_w_out,
        'conv_w_in': conv_w_in, 'conv_b_in': conv_b_in, 'conv_w_dw': conv_w_dw, 'conv_b_dw': conv_b_dw,
        'conv_ln_g': conv_ln_g, 'conv_ln_b': conv_ln_b, 'conv_w_out': conv_w_out,
        'conv_b_out': conv_b_out,
    }
    depth, d, _ = w_mod.shape
    b_p, b_s = x_prompt.shape[0], x_sample.shape[0]
    n_gla, _, heads, dk, dv = state_gla.shape
    n_conv, _, hist, cdim = cache_conv.shape

    c_all = jnp.concatenate([c_prompt, c_sample, jnp.zeros((MOD_ROWS - b_p - b_s, d), F32)], axis=0)
    mod = _modulation(c_all, w_mod, b_mod)
    mod_p = mod[:, :b_p].reshape(depth, b_p, N_MOD, d)
    mod_s = mod[:, b_p:b_p + b_s].reshape(depth, b_s, N_MOD, d)

    gla0 = jnp.zeros((n_gla, b_p, heads, dk, dv), F32)
    conv0 = jnp.zeros((n_conv, b_p, hist, cdim), F32)
    y_p, gla_p, conv_p = _run_trunk(x_prompt, mod_p, gla0, conv0, p)
    y_s, gla_s, conv_s = _run_trunk(x_sample, mod_s, state_gla, cache_conv, p)
    return (y_p, y_s, gla_p, conv_p, gla_s, conv_s)
```

```python
import functools

import jax
import jax.numpy as jnp
from jax import lax
from jax.experimental import pallas as pl
from jax.experimental.pallas import tpu as pltpu

F32 = jnp.float32
BF16 = jnp.bfloat16

CHUNK = 64
GLA_HEADS = 4
GATE_TAU = 16.0
CONV_WIDTH = 31
N_MOD = 6
EPS = 1e-6

SUBLANES = 8
HIST_ROWS = 32
HIST_PAD = HIST_ROWS - (CONV_WIDTH - 1)
CONV_ROW_BLOCK = 64
CONV_LANE_BLOCK = 128
SHIFT_ROW_BLOCK = 32
FF_TILE = 1024
ROW_TILE = 512
VMEM_LIMIT_BYTES = 56 * 1024 * 1024
MOD_ROWS = 16
MOD_COL_TILE = 1536


def _sigmoid(x):
    return 1.0 / (1.0 + jnp.exp(-x))


def _silu(x):
    return x * _sigmoid(x)


def _log_sigmoid(x):
    return jnp.minimum(x, 0.0) - jnp.log(1.0 + jnp.exp(-jnp.abs(x)))


def _rms(x, g):
    return x * lax.rsqrt(jnp.mean(x * x, axis=-1, keepdims=True) + EPS) * g


def _layer_norm_silu(y, g, b):
    mu = jnp.mean(y, axis=-1, keepdims=True)
    yc = y - mu
    var = jnp.mean(yc * yc, axis=-1, keepdims=True)
    return _silu(yc * lax.rsqrt(var + EPS) * g + b)


def _dot(a, b):
    return jnp.dot(a, b, preferred_element_type=F32)


def _modulated_norm(x3, g, shift, scale):
    bb, tm, d = x3.shape
    h = _rms(x3.reshape(bb * tm, d), g).reshape(bb, tm, d) * (1.0 + scale) + shift
    return h.reshape(bb * tm, d).astype(BF16)


def _gated_residual(x3, y2, g, gate):
    bb, tm, d = x3.shape
    return x3 + gate * _rms(y2, g).reshape(bb, tm, d)


def _resident_spec(shape):
    nd = len(shape)
    return pl.BlockSpec(shape, lambda *_: (0,) * nd, pipeline_mode=pl.Buffered(1))


_GRID2 = pltpu.CompilerParams(dimension_semantics=("arbitrary", "arbitrary"), vmem_limit_bytes=VMEM_LIMIT_BYTES)


def _mod_kernel(c_ref, w_ref, b_ref, o_ref):
    s = _silu(c_ref[...]).astype(BF16)
    o_ref[0] = _dot(s, w_ref[0].astype(BF16)) + b_ref[0]


def _modulation(c_all, w_mod, b_mod):
    depth, d, n = w_mod.shape
    return pl.pallas_call(
        _mod_kernel,
        grid=(depth, n // MOD_COL_TILE),
        in_specs=[
            pl.BlockSpec((MOD_ROWS, d), lambda i, j: (0, 0)),
            pl.BlockSpec((1, d, MOD_COL_TILE), lambda i, j: (i, 0, j)),
            pl.BlockSpec((1, 1, MOD_COL_TILE), lambda i, j: (i, 0, j)),
        ],
        out_specs=pl.BlockSpec((1, MOD_ROWS, MOD_COL_TILE), lambda i, j: (i, 0, j)),
        out_shape=jax.ShapeDtypeStruct((depth, MOD_ROWS, n), F32),
        compiler_params=_GRID2,
        name="modulation",
    )(c_all, w_mod, b_mod.reshape(depth, 1, n))


def _chunk_cumsum(g, bcum_scr):
    m, lanes = g.shape
    x = g.reshape(m // SUBLANES, SUBLANES, lanes)
    sub = lax.broadcasted_iota(jnp.int32, (1, SUBLANES, lanes), 1)
    s = 1
    while s < SUBLANES:
        x = x + jnp.where(sub >= s, pltpu.roll(x, s, axis=1), 0.0)
        s *= 2
    groups = CHUNK // SUBLANES
    totals = []
    for ci in range(m // CHUNK):
        carry = None
        for gi in range(groups):
            idx = ci * groups + gi
            blk = x[idx] if carry is None else x[idx] + carry
            bcum_scr[idx * SUBLANES:(idx + 1) * SUBLANES, :] = blk
            carry = jnp.broadcast_to(blk[SUBLANES - 1:SUBLANES, :], (SUBLANES, lanes))
        totals.append(carry)
    return totals


def _gla_kernel(x_ref, mod_ref, s0_ref, npre_ref, win_ref, wga_ref, wgb_ref, bg_ref, gn_ref, wout_ref,
                npost_ref, xo_ref, st_ref, proj_scr, bcum_scr, qd_scr, ki_scr, ke_scr, dexp_scr, o_scr, u_scr, sc_scr,
                *, bb, tm):
    kdim = wgb_ref.shape[-1]
    dk = kdim // GLA_HEADS
    vdim = wout_ref.shape[0]
    dv = vdim // GLA_HEADS
    rows_total = bb * tm
    n_chunks = tm // CHUNK
    chunks = [slice(ci * CHUNK, (ci + 1) * CHUNK) for ci in range(rows_total // CHUNK)]
    heads = [(slice(h * dk, (h + 1) * dk), slice(h * dv, (h + 1) * dv)) for h in range(GLA_HEADS)]

    @pl.when(pl.program_id(1) == 0)
    def _():
        st_ref[...] = s0_ref[...]
        dexp_scr[...] = jnp.zeros(dexp_scr.shape, F32)

    x3 = x_ref[...]
    m = mod_ref[...]
    hb = _modulated_norm(x3, npre_ref[...], m[:, 0:1, :], m[:, 1:2, :])
    ga = _dot(hb, wga_ref[...]).astype(BF16)
    gl = _dot(ga, wgb_ref[...]) + bg_ref[...]
    totals = _chunk_cumsum(_log_sigmoid(gl) / GATE_TAU, bcum_scr)
    proj_scr[...] = _dot(hb, win_ref[...].astype(BF16))

    q_scale = dk ** -0.5
    for ci, rows in enumerate(chunks):
        bcum = bcum_scr[rows, :]
        blast = totals[ci][0:1, :]
        q = proj_scr[rows, 0:kdim]
        k = proj_scr[rows, kdim:2 * kdim]
        qd_scr[rows, :] = ((q * q_scale) * jnp.exp(bcum)).astype(BF16)
        ki_scr[rows, :] = (k * jnp.exp(-bcum)).astype(BF16)
        ke_scr[rows, :] = k * jnp.exp(blast - bcum)
        dexp_scr[ci:ci + 1, :] = jnp.exp(blast)
    decay_cols = dexp_scr[...].T

    row = lax.broadcasted_iota(jnp.int32, (CHUNK, CHUNK), 0)
    col = lax.broadcasted_iota(jnp.int32, (CHUNK, CHUNK), 1)
    causal = col <= row
    for ci, rows in enumerate(chunks):
        for h, (ks, vs) in enumerate(heads):
            scores = lax.dot_general(qd_scr[rows, ks], ki_scr[rows, ks], (((1,), (1,)), ((), ())),
                                     preferred_element_type=F32)
            sc_scr[ci, h] = jnp.where(causal, scores, 0.0).astype(BF16)
    for ci, rows in enumerate(chunks):
        for h, (ks, vs) in enumerate(heads):
            vh = proj_scr[rows, 2 * kdim + h * dv:2 * kdim + (h + 1) * dv].astype(BF16)
            keh_t = ke_scr[rows, ks].T.astype(BF16)
            both = _dot(jnp.concatenate([sc_scr[ci, h], keh_t], axis=0), vh)
            o_scr[rows, vs] = both[:CHUNK]
            u_scr[ci, h] = both[CHUNK:]

    gn = gn_ref[...]
    for ci, rows in enumerate(chunks):
        b = ci // n_chunks
        for h, (ks, vs) in enumerate(heads):
            s = st_ref[b, h]
            o = o_scr[rows, vs] + _dot(qd_scr[rows, ks], s.astype(BF16))
            o_scr[rows, vs] = _rms(o, gn)
            st_ref[b, h] = decay_cols[ks, ci:ci + 1] * s + u_scr[ci, h]

    r = proj_scr[:, 2 * kdim + vdim:]
    z = (o_scr[...] * _silu(r)).astype(BF16)
    y = _dot(z, wout_ref[...].astype(BF16))
    xo_ref[...] = _gated_residual(x3, y, npost_ref[...], m[:, 2:3, :])


def _gla_layer(x, mod, s0, npre, w_in, w_ga, w_gb, b_g, gnorm, w_out, npost, *, bb, tm):
    bsz, t, d = x.shape
    _, heads, dk, dv = s0.shape
    pdim = w_in.shape[1]
    kdim, vdim = heads * dk, heads * dv
    rows = bb * tm
    chunks_pad = -(-(rows // CHUNK) // SUBLANES) * SUBLANES
    xspec = pl.BlockSpec((bb, tm, d), lambda i, j: (i, j, 0))
    sspec = pl.BlockSpec((bb, heads, dk, dv), lambda i, j: (i, 0, 0, 0))
    return pl.pallas_call(
        functools.partial(_gla_kernel, bb=bb, tm=tm),
        grid=(bsz // bb, t // tm),
        in_specs=[
            xspec,
            pl.BlockSpec((bb, N_MOD, d), lambda i, j: (i, 0, 0)),
            sspec,
            _resident_spec((1, d)),
            _resident_spec((d, pdim)),
            _resident_spec(w_ga.shape),
            _resident_spec(w_gb.shape),
            _resident_spec((1, kdim)),
            _resident_spec((1, dv)),
            _resident_spec((vdim, d)),
            _resident_spec((1, d)),
        ],
        out_specs=[xspec, sspec],
        out_shape=[jax.ShapeDtypeStruct(x.shape, F32), jax.ShapeDtypeStruct(s0.shape, F32)],
        scratch_shapes=[
            pltpu.VMEM((rows, pdim), F32),
            pltpu.VMEM((rows, kdim), F32),
            pltpu.VMEM((rows, kdim), BF16),
            pltpu.VMEM((rows, kdim), BF16),
            pltpu.VMEM((rows, kdim), F32),
            pltpu.VMEM((chunks_pad, kdim), F32),
            pltpu.VMEM((rows, vdim), F32),
            pltpu.VMEM((rows // CHUNK, heads, dk, dv), F32),
            pltpu.VMEM((rows // CHUNK, heads, CHUNK, CHUNK), BF16),
        ],
        compiler_params=_GRID2,
        name="gla_mixer",
    )(x, mod, s0, npre.reshape(1, d), w_in, w_ga, w_gb, b_g.reshape(1, kdim), gnorm.reshape(1, dv), w_out,
      npost.reshape(1, d))


def _conv_kernel(x_ref, mod_ref, cache_ref, npre_ref, win_ref, bin_ref, wdw_ref, bdw_ref, lng_ref, lnb_ref,
                 wout_ref, bout_ref, npost_ref, xo_ref, cache_out_ref, ext_scr, shift_scr, wtap_scr, y_scr,
                 *, bb, tm):
    c = wdw_ref.shape[-1]
    ext_rows = HIST_ROWS + tm

    @pl.when(pl.program_id(1) == 0)
    def _():
        ext_scr[:, 0:HIST_ROWS, :] = cache_ref[...]
        ext_scr[:, ext_rows:, :] = jnp.zeros((bb, SUBLANES, c), F32)
        for j in range(CONV_WIDTH):
            wtap_scr[j * SUBLANES:(j + 1) * SUBLANES, :] = jnp.broadcast_to(wdw_ref[j:j + 1, :], (SUBLANES, c))

    x3 = x_ref[...]
    m = mod_ref[...]
    hb = _modulated_norm(x3, npre_ref[...], m[:, 0:1, :], m[:, 1:2, :])
    u = _dot(hb, win_ref[...]) + bin_ref[...]
    glu = u[:, :c] * _sigmoid(u[:, c:])
    ext_scr[:, HIST_ROWS:ext_rows, :] = glu.reshape(bb, tm, c)

    n_shift_blocks = ext_rows // SHIFT_ROW_BLOCK

    def shift_step(idx, carry):
        b = idx // n_shift_blocks
        r0 = pl.multiple_of((idx % n_shift_blocks) * SHIFT_ROW_BLOCK, SHIFT_ROW_BLOCK)
        groups = SHIFT_ROW_BLOCK // SUBLANES
        window = ext_scr[b, pl.ds(r0, SHIFT_ROW_BLOCK + SUBLANES), :].reshape(groups + 1, SUBLANES, c)
        sub = lax.broadcasted_iota(jnp.int32, (1, SUBLANES, c), 1)
        for s in range(1, SUBLANES):
            mixed = jnp.where(sub >= s, window[:groups], window[1:])
            shifted = pltpu.roll(mixed, SUBLANES - s, axis=1)
            shift_scr[s - 1, b, pl.ds(r0, SHIFT_ROW_BLOCK), :] = shifted.reshape(SHIFT_ROW_BLOCK, c)
        return carry

    lax.fori_loop(0, bb * n_shift_blocks, shift_step, 0)

    n_row_blocks = tm // CONV_ROW_BLOCK

    def tap_step(idx, carry):
        b = idx // n_row_blocks
        r0 = pl.multiple_of((idx % n_row_blocks) * CONV_ROW_BLOCK, CONV_ROW_BLOCK)
        groups = CONV_ROW_BLOCK // SUBLANES
        for l0 in range(0, c, CONV_LANE_BLOCK):
            lanes = slice(l0, l0 + CONV_LANE_BLOCK)
            acc = jnp.zeros((CONV_ROW_BLOCK, CONV_LANE_BLOCK), F32)
            for s in range(SUBLANES):
                taps = [j for j in range(CONV_WIDTH) if (HIST_PAD + j) % SUBLANES == s]
                first = (HIST_PAD + taps[0]) // SUBLANES * SUBLANES
                span = (HIST_PAD + taps[-1]) // SUBLANES * SUBLANES - first + CONV_ROW_BLOCK
                rows = pl.ds(r0 + first, span)
                window = ext_scr[b, rows, lanes] if s == 0 else shift_scr[s - 1, b, rows, lanes]
                for j in taps:
                    a = (HIST_PAD + j) // SUBLANES * SUBLANES - first
                    w8 = wtap_scr[j * SUBLANES:(j + 1) * SUBLANES, lanes]
                    src = window[a:a + CONV_ROW_BLOCK].reshape(groups, SUBLANES, CONV_LANE_BLOCK)
                    acc = acc + (w8[None] * src).reshape(CONV_ROW_BLOCK, CONV_LANE_BLOCK)
            y_scr[b, pl.ds(r0, CONV_ROW_BLOCK), lanes] = acc
        return carry

    lax.fori_loop(0, bb * n_row_blocks, tap_step, 0)

    tail = ext_scr[:, tm:ext_rows, :]
    cache_out_ref[...] = tail
    ext_scr[:, 0:HIST_ROWS, :] = tail

    z = _layer_norm_silu(y_scr[...].reshape(bb * tm, c) + bdw_ref[...], lng_ref[...], lnb_ref[...]).astype(BF16)
    out = _dot(z, wout_ref[...]) + bout_ref[...]
    xo_ref[...] = _gated_residual(x3, out, npost_ref[...], m[:, 2:3, :])


def _conv_layer(x, mod, cache, npre, w_in, b_in, w_dw, b_dw, ln_g, ln_b, w_out, b_out, npost, *, bb, tm):
    bsz, t, d = x.shape
    c = w_dw.shape[-1]
    assert tm % CONV_ROW_BLOCK == 0 and tm >= HIST_ROWS
    xspec = pl.BlockSpec((bb, tm, d), lambda i, j: (i, j, 0))
    cspec = pl.BlockSpec((bb, HIST_ROWS, c), lambda i, j: (i, 0, 0))
    cache32 = jnp.pad(cache, ((0, 0), (HIST_PAD, 0), (0, 0)))
    x_new, cache_new = pl.pallas_call(
        functools.partial(_conv_kernel, bb=bb, tm=tm),
        grid=(bsz // bb, t // tm),
        in_specs=[
            xspec,
            pl.BlockSpec((bb, N_MOD, d), lambda i, j: (i, 0, 0)),
            cspec,
            _resident_spec((1, d)),
            _resident_spec(w_in.shape),
            _resident_spec((1, 2 * c)),
            _resident_spec(w_dw.shape),
            _resident_spec((1, c)),
            _resident_spec((1, c)),
            _resident_spec((1, c)),
            _resident_spec(w_out.shape),
            _resident_spec((1, d)),
            _resident_spec((1, d)),
        ],
        out_specs=[xspec, cspec],
        out_shape=[jax.ShapeDtypeStruct(x.shape, F32), jax.ShapeDtypeStruct((bsz, HIST_ROWS, c), F32)],
        scratch_shapes=[
            pltpu.VMEM((bb, HIST_ROWS + tm + SUBLANES, c), F32),
            pltpu.VMEM((SUBLANES - 1, bb, HIST_ROWS + tm, c), F32),
            pltpu.VMEM((CONV_WIDTH * SUBLANES, c), F32),
            pltpu.VMEM((bb, tm, c), F32),
        ],
        compiler_params=_GRID2,
        name="conv_mixer",
    )(x, mod, cache32, npre.reshape(1, d), w_in, b_in.reshape(1, 2 * c), w_dw, b_dw.reshape(1, c),
      ln_g.reshape(1, c), ln_b.reshape(1, c), w_out, b_out.reshape(1, d), npost.reshape(1, d))
    return x_new, cache_new[:, HIST_PAD:, :]


def _ffn_kernel(x_ref, mod_ref, npre_ref, wup_ref, wdn_ref, npost_ref, xo_ref):
    x3 = x_ref[...]
    bb, tm, d = x3.shape
    m = mod_ref[...]
    hb = _modulated_norm(x3, npre_ref[...], m[:, 3:4, :], m[:, 4:5, :])
    y = jnp.zeros((bb * tm, d), F32)
    for f0 in range(0, wup_ref.shape[1], FF_TILE):
        up = _dot(hb, wup_ref[:, f0:f0 + FF_TILE])
        y = y + _dot(jnp.square(jnp.maximum(up, 0.0)).astype(BF16), wdn_ref[f0:f0 + FF_TILE, :])
    xo_ref[...] = _gated_residual(x3, y, npost_ref[...], m[:, 5:6, :])


def _ffn_layer(x, mod, npre, w_up, w_dn, npost, *, bb, tm):
    bsz, t, d = x.shape
    xspec = pl.BlockSpec((bb, tm, d), lambda i, j: (i, j, 0))
    return pl.pallas_call(
        _ffn_kernel,
        grid=(bsz // bb, t // tm),
        in_specs=[
            xspec,
            pl.BlockSpec((bb, N_MOD, d), lambda i, j: (i, 0, 0)),
            _resident_spec((1, d)),
            _resident_spec(w_up.shape),
            _resident_spec(w_dn.shape),
            _resident_spec((1, d)),
        ],
        out_specs=xspec,
        out_shape=jax.ShapeDtypeStruct(x.shape, F32),
        compiler_params=_GRID2,
        name="ffn",
    )(x, mod, npre.reshape(1, d), w_up, w_dn, npost.reshape(1, d))


def _tiling(bsz, t):
    tm = min(t, ROW_TILE)
    bb = max(1, min(bsz, ROW_TILE // tm))
    assert t % tm == 0 and bsz % bb == 0 and tm % CHUNK == 0
    return bb, tm


def _run_trunk(x, mod, gla_states, conv_caches, p):
    bb, tm = _tiling(x.shape[0], x.shape[1])
    depth = len(p['w_ffn_up'])
    new_gla, new_conv = [], []
    for i in range(depth):
        j = i // 2
        if i % 2 == 0:
            x, s = _gla_layer(x, mod[i], gla_states[j], p['norm_mix_pre'][i], p['gla_w_in'][j], p['gla_w_gate_a'][j],
                              p['gla_w_gate_b'][j], p['gla_b_gate'][j], p['gla_norm'][j], p['gla_w_out'][j],
                              p['norm_mix_post'][i], bb=bb, tm=tm)
            new_gla.append(s)
        else:
            x, s = _conv_layer(x, mod[i], conv_caches[j], p['norm_mix_pre'][i], p['conv_w_in'][j], p['conv_b_in'][j],
                               p['conv_w_dw'][j], p['conv_b_dw'][j], p['conv_ln_g'][j], p['conv_ln_b'][j],
                               p['conv_w_out'][j], p['conv_b_out'][j], p['norm_mix_post'][i], bb=bb, tm=tm)
            new_conv.append(s)
        x = _ffn_layer(x, mod[i], p['norm_ffn_pre'][i], p['w_ffn_up'][i], p['w_ffn_down'][i], p['norm_ffn_post'][i],
                       bb=bb, tm=tm)
    return x, jnp.stack(new_gla), jnp.stack(new_conv)


def kernel(x_prompt, x_sample, c_prompt, c_sample, state_gla, cache_conv, w_mod, b_mod, norm_mix_pre, norm_mix_post, norm_ffn_pre, norm_ffn_post, w_ffn_up, w_ffn_down, gla_w_in, gla_w_gate_a, gla_w_gate_b, gla_b_gate, gla_norm, gla_w_out, conv_w_in, conv_b_in, conv_w_dw, conv_b_dw, conv_ln_g, conv_ln_b, conv_w_out, conv_b_out):
    p = {
        'norm_mix_pre': norm_mix_pre, 'norm_mix_post': norm_mix_post,
        'norm_ffn_pre': norm_ffn_pre, 'norm_ffn_post': norm_ffn_post,
        'w_ffn_up': [w.astype(BF16) for w in w_ffn_up], 'w_ffn_down': [w.astype(BF16) for w in w_ffn_down],
        'gla_w_in': gla_w_in, 'gla_w_gate_a': gla_w_gate_a.astype(BF16),
        'gla_w_gate_b': gla_w_gate_b.astype(BF16), 'gla_b_gate': gla_b_gate, 'gla_norm': gla_norm,
        'gla_w_out': gla_w_out,
        'conv_w_in': conv_w_in.astype(BF16), 'conv_b_in': conv_b_in, 'conv_w_dw': conv_w_dw, 'conv_b_dw': conv_b_dw,
        'conv_ln_g': conv_ln_g, 'conv_ln_b': conv_ln_b, 'conv_w_out': conv_w_out.astype(BF16),
        'conv_b_out': conv_b_out,
    }
    depth, d, _ = w_mod.shape
    b_p, b_s = x_prompt.shape[0], x_sample.shape[0]
    n_gla, _, heads, dk, dv = state_gla.shape
    n_conv, _, hist, cdim = cache_conv.shape

    c_all = jnp.concatenate([c_prompt, c_sample, jnp.zeros((MOD_ROWS - b_p - b_s, d), F32)], axis=0)
    mod = _modulation(c_all, w_mod, b_mod)
    mod_p = mod[:, :b_p].reshape(depth, b_p, N_MOD, d)
    mod_s = mod[:, b_p:b_p + b_s].reshape(depth, b_s, N_MOD, d)

    gla0 = jnp.zeros((n_gla, b_p, heads, dk, dv), F32)
    conv0 = jnp.zeros((n_conv, b_p, hist, cdim), F32)
    y_p, gla_p, conv_p = _run_trunk(x_prompt, mod_p, gla0, conv0, p)
    y_s, gla_s, conv_s = _run_trunk(x_sample, mod_s, state_gla, cache_conv, p)
    return (y_p, y_s, gla_p, conv_p, gla_s, conv_s)
```

```python
import functools

import jax
import jax.numpy as jnp
from jax import lax
from jax.experimental import pallas as pl
from jax.experimental.pallas import tpu as pltpu

F32 = jnp.float32
BF16 = jnp.bfloat16

CHUNK = 64
GLA_HEADS = 4
GATE_TAU = 16.0
CONV_WIDTH = 31
N_MOD = 6
EPS = 1e-6

SUBLANES = 8
HIST_ROWS = 32
HIST_PAD = HIST_ROWS - (CONV_WIDTH - 1)
CONV_ROW_BLOCK = 64
CONV_LANE_BLOCK = 128
SHIFT_ROW_BLOCK = 32
FF_TILE = 1024
ROW_TILE = 512
VMEM_LIMIT_BYTES = 56 * 1024 * 1024
MOD_ROWS = 16
MOD_COL_TILE = 1536


def _sigmoid(x):
    return 1.0 / (1.0 + jnp.exp(-x))


def _silu(x):
    return x * _sigmoid(x)


def _log_sigmoid(x):
    return jnp.minimum(x, 0.0) - jnp.log(1.0 + jnp.exp(-jnp.abs(x)))


def _rms(x, g):
    return x * lax.rsqrt(jnp.mean(x * x, axis=-1, keepdims=True) + EPS) * g


def _layer_norm_silu(y, g, b):
    mu = jnp.mean(y, axis=-1, keepdims=True)
    yc = y - mu
    var = jnp.mean(yc * yc, axis=-1, keepdims=True)
    return _silu(yc * lax.rsqrt(var + EPS) * g + b)


def _dot(a, b):
    return jnp.dot(a, b, preferred_element_type=F32)


def _modulated_norm(x3, g, shift, scale):
    bb, tm, d = x3.shape
    h = _rms(x3.reshape(bb * tm, d), g).reshape(bb, tm, d) * (1.0 + scale) + shift
    return h.reshape(bb * tm, d).astype(BF16)


def _gated_residual(x3, y2, g, gate):
    bb, tm, d = x3.shape
    return x3 + gate * _rms(y2, g).reshape(bb, tm, d)


def _resident_spec(shape):
    nd = len(shape)
    return pl.BlockSpec(shape, lambda *_: (0,) * nd, pipeline_mode=pl.Buffered(1))


_GRID2 = pltpu.CompilerParams(dimension_semantics=("arbitrary", "arbitrary"), vmem_limit_bytes=VMEM_LIMIT_BYTES)


def _mod_kernel(c_ref, w_ref, b_ref, o_ref):
    s = _silu(c_ref[...]).astype(BF16)
    o_ref[0] = _dot(s, w_ref[0].astype(BF16)) + b_ref[0]


def _modulation(c_all, w_mod, b_mod):
    depth, d, n = w_mod.shape
    return pl.pallas_call(
        _mod_kernel,
        grid=(depth, n // MOD_COL_TILE),
        in_specs=[
            pl.BlockSpec((MOD_ROWS, d), lambda i, j: (0, 0)),
            pl.BlockSpec((1, d, MOD_COL_TILE), lambda i, j: (i, 0, j)),
            pl.BlockSpec((1, 1, MOD_COL_TILE), lambda i, j: (i, 0, j)),
        ],
        out_specs=pl.BlockSpec((1, MOD_ROWS, MOD_COL_TILE), lambda i, j: (i, 0, j)),
        out_shape=jax.ShapeDtypeStruct((depth, MOD_ROWS, n), F32),
        compiler_params=_GRID2,
        name="modulation",
    )(c_all, w_mod, b_mod.reshape(depth, 1, n))


def _chunk_cumsum(g, bcum_scr):
    m, lanes = g.shape
    x = g.reshape(m // SUBLANES, SUBLANES, lanes)
    sub = lax.broadcasted_iota(jnp.int32, (1, SUBLANES, lanes), 1)
    s = 1
    while s < SUBLANES:
        x = x + jnp.where(sub >= s, pltpu.roll(x, s, axis=1), 0.0)
        s *= 2
    groups = CHUNK // SUBLANES
    totals = []
    for ci in range(m // CHUNK):
        carry = None
        for gi in range(groups):
            idx = ci * groups + gi
            blk = x[idx] if carry is None else x[idx] + carry
            bcum_scr[idx * SUBLANES:(idx + 1) * SUBLANES, :] = blk
            carry = jnp.broadcast_to(blk[SUBLANES - 1:SUBLANES, :], (SUBLANES, lanes))
        totals.append(carry)
    return totals


def _gla_kernel(x_ref, mod_ref, s0_ref, npre_ref, win_ref, wga_ref, wgb_ref, bg_ref, gn_ref, wout_ref,
                npost_ref, xo_ref, st_ref, proj_scr, bcum_scr, qd_scr, ki_scr, ke_scr, dexp_scr, o_scr, u_scr, sc_scr,
                *, bb, tm):
    kdim = wgb_ref.shape[-1]
    dk = kdim // GLA_HEADS
    vdim = wout_ref.shape[0]
    dv = vdim // GLA_HEADS
    rows_total = bb * tm
    n_chunks = tm // CHUNK
    chunks = [slice(ci * CHUNK, (ci + 1) * CHUNK) for ci in range(rows_total // CHUNK)]
    heads = [(slice(h * dk, (h + 1) * dk), slice(h * dv, (h + 1) * dv)) for h in range(GLA_HEADS)]

    @pl.when(pl.program_id(1) == 0)
    def _():
        st_ref[...] = s0_ref[...]
        dexp_scr[...] = jnp.zeros(dexp_scr.shape, F32)

    x3 = x_ref[...]
    m = mod_ref[...]
    hb = _modulated_norm(x3, npre_ref[...], m[:, 0:1, :], m[:, 1:2, :])
    ga = _dot(hb, wga_ref[...]).astype(BF16)
    gl = _dot(ga, wgb_ref[...]) + bg_ref[...]
    totals = _chunk_cumsum(_log_sigmoid(gl) / GATE_TAU, bcum_scr)
    proj_scr[...] = _dot(hb, win_ref[...].astype(BF16))

    q_scale = dk ** -0.5
    for ci, rows in enumerate(chunks):
        bcum = bcum_scr[rows, :]
        blast = totals[ci][0:1, :]
        q = proj_scr[rows, 0:kdim]
        k = proj_scr[rows, kdim:2 * kdim]
        qd_scr[rows, :] = ((q * q_scale) * jnp.exp(bcum)).astype(BF16)
        ki_scr[rows, :] = (k * jnp.exp(-bcum)).astype(BF16)
        ke_scr[rows, :] = k * jnp.exp(blast - bcum)
        dexp_scr[ci:ci + 1, :] = jnp.exp(blast)
    decay_cols = dexp_scr[...].T

    row = lax.broadcasted_iota(jnp.int32, (CHUNK, CHUNK), 0)
    col = lax.broadcasted_iota(jnp.int32, (CHUNK, CHUNK), 1)
    causal = col <= row
    for ci, rows in enumerate(chunks):
        for h, (ks, vs) in enumerate(heads):
            scores = lax.dot_general(qd_scr[rows, ks], ki_scr[rows, ks], (((1,), (1,)), ((), ())),
                                     preferred_element_type=F32)
            sc_scr[ci, h] = jnp.where(causal, scores, 0.0).astype(BF16)
    for ci, rows in enumerate(chunks):
        for h, (ks, vs) in enumerate(heads):
            vh = proj_scr[rows, 2 * kdim + h * dv:2 * kdim + (h + 1) * dv].astype(BF16)
            keh_t = ke_scr[rows, ks].T.astype(BF16)
            both = _dot(jnp.concatenate([sc_scr[ci, h], keh_t], axis=0), vh)
            o_scr[rows, vs] = both[:CHUNK]
            u_scr[ci, h] = both[CHUNK:]

    gn = gn_ref[...]
    for ci, rows in enumerate(chunks):
        b = ci // n_chunks
        for h, (ks, vs) in enumerate(heads):
            s = st_ref[b, h]
            o = o_scr[rows, vs] + _dot(qd_scr[rows, ks], s.astype(BF16))
            o_scr[rows, vs] = _rms(o, gn)
            st_ref[b, h] = decay_cols[ks, ci:ci + 1] * s + u_scr[ci, h]

    r = proj_scr[:, 2 * kdim + vdim:]
    z = (o_scr[...] * _silu(r)).astype(BF16)
    y = _dot(z, wout_ref[...].astype(BF16))
    xo_ref[...] = _gated_residual(x3, y, npost_ref[...], m[:, 2:3, :])


def _gla_layer(x, mod, s0, npre, w_in, w_ga, w_gb, b_g, gnorm, w_out, npost, *, bb, tm):
    bsz, t, d = x.shape
    _, heads, dk, dv = s0.shape
    pdim = w_in.shape[1]
    kdim, vdim = heads * dk, heads * dv
    rows = bb * tm
    chunks_pad = -(-(rows // CHUNK) // SUBLANES) * SUBLANES
    xspec = pl.BlockSpec((bb, tm, d), lambda i, j: (i, j, 0))
    sspec = pl.BlockSpec((bb, heads, dk, dv), lambda i, j: (i, 0, 0, 0))
    return pl.pallas_call(
        functools.partial(_gla_kernel, bb=bb, tm=tm),
        grid=(bsz // bb, t // tm),
        in_specs=[
            xspec,
            pl.BlockSpec((bb, N_MOD, d), lambda i, j: (i, 0, 0)),
            sspec,
            _resident_spec((1, d)),
            _resident_spec((d, pdim)),
            _resident_spec(w_ga.shape),
            _resident_spec(w_gb.shape),
            _resident_spec((1, kdim)),
            _resident_spec((1, dv)),
            _resident_spec((vdim, d)),
            _resident_spec((1, d)),
        ],
        out_specs=[xspec, sspec],
        out_shape=[jax.ShapeDtypeStruct(x.shape, F32), jax.ShapeDtypeStruct(s0.shape, F32)],
        scratch_shapes=[
            pltpu.VMEM((rows, pdim), F32),
            pltpu.VMEM((rows, kdim), F32),
            pltpu.VMEM((rows, kdim), BF16),
            pltpu.VMEM((rows, kdim), BF16),
            pltpu.VMEM((rows, kdim), F32),
            pltpu.VMEM((chunks_pad, kdim), F32),
            pltpu.VMEM((rows, vdim), F32),
            pltpu.VMEM((rows // CHUNK, heads, dk, dv), F32),
            pltpu.VMEM((rows // CHUNK, heads, CHUNK, CHUNK), BF16),
        ],
        compiler_params=_GRID2,
        name="gla_mixer",
    )(x, mod, s0, npre.reshape(1, d), w_in, w_ga, w_gb, b_g.reshape(1, kdim), gnorm.reshape(1, dv), w_out,
      npost.reshape(1, d))


def _conv_kernel(x_ref, mod_ref, cache_ref, npre_ref, win_ref, bin_ref, wdw_ref, bdw_ref, lng_ref, lnb_ref,
                 wout_ref, bout_ref, npost_ref, xo_ref, cache_out_ref, ext_scr, shift_scr, wtap_scr, y_scr,
                 *, bb, tm):
    c = wdw_ref.shape[-1]
    ext_rows = HIST_ROWS + tm

    @pl.when(pl.program_id(1) == 0)
    def _():
        ext_scr[:, 0:HIST_ROWS, :] = cache_ref[...]
        ext_scr[:, ext_rows:, :] = jnp.zeros((bb, SUBLANES, c), F32)
        for j in range(CONV_WIDTH):
            wtap_scr[j * SUBLANES:(j + 1) * SUBLANES, :] = jnp.broadcast_to(wdw_ref[j:j + 1, :], (SUBLANES, c))

    x3 = x_ref[...]
    m = mod_ref[...]
    hb = _modulated_norm(x3, npre_ref[...], m[:, 0:1, :], m[:, 1:2, :])
    gate = _sigmoid(_dot(hb, win_ref[:, c:]) + bin_ref[:, c:])
    glu = (_dot(hb, win_ref[:, :c]) + bin_ref[:, :c]) * gate
    ext_scr[:, HIST_ROWS:ext_rows, :] = glu.reshape(bb, tm, c)

    n_shift_blocks = ext_rows // SHIFT_ROW_BLOCK

    def shift_step(idx, carry):
        b = idx // n_shift_blocks
        r0 = pl.multiple_of((idx % n_shift_blocks) * SHIFT_ROW_BLOCK, SHIFT_ROW_BLOCK)
        groups = SHIFT_ROW_BLOCK // SUBLANES
        window = ext_scr[b, pl.ds(r0, SHIFT_ROW_BLOCK + SUBLANES), :].reshape(groups + 1, SUBLANES, c)
        sub = lax.broadcasted_iota(jnp.int32, (1, SUBLANES, c), 1)
        for s in range(1, SUBLANES):
            mixed = jnp.where(sub >= s, window[:groups], window[1:])
            shifted = pltpu.roll(mixed, SUBLANES - s, axis=1)
            shift_scr[s - 1, b, pl.ds(r0, SHIFT_ROW_BLOCK), :] = shifted.reshape(SHIFT_ROW_BLOCK, c)
        return carry

    lax.fori_loop(0, bb * n_shift_blocks, shift_step, 0)

    n_row_blocks = tm // CONV_ROW_BLOCK

    def tap_step(idx, carry):
        b = idx // n_row_blocks
        r0 = pl.multiple_of((idx % n_row_blocks) * CONV_ROW_BLOCK, CONV_ROW_BLOCK)
        groups = CONV_ROW_BLOCK // SUBLANES
        for l0 in range(0, c, CONV_LANE_BLOCK):
            lanes = slice(l0, l0 + CONV_LANE_BLOCK)
            acc = jnp.zeros((CONV_ROW_BLOCK, CONV_LANE_BLOCK), F32)
            for s in range(SUBLANES):
                taps = [j for j in range(CONV_WIDTH) if (HIST_PAD + j) % SUBLANES == s]
                first = (HIST_PAD + taps[0]) // SUBLANES * SUBLANES
                span = (HIST_PAD + taps[-1]) // SUBLANES * SUBLANES - first + CONV_ROW_BLOCK
                rows = pl.ds(r0 + first, span)
                window = ext_scr[b, rows, lanes] if s == 0 else shift_scr[s - 1, b, rows, lanes]
                for j in taps:
                    a = (HIST_PAD + j) // SUBLANES * SUBLANES - first
                    w8 = wtap_scr[j * SUBLANES:(j + 1) * SUBLANES, lanes]
                    src = window[a:a + CONV_ROW_BLOCK].reshape(groups, SUBLANES, CONV_LANE_BLOCK)
                    acc = acc + (w8[None] * src).reshape(CONV_ROW_BLOCK, CONV_LANE_BLOCK)
            y_scr[b, pl.ds(r0, CONV_ROW_BLOCK), lanes] = acc
        return carry

    lax.fori_loop(0, bb * n_row_blocks, tap_step, 0)

    tail = ext_scr[:, tm:ext_rows, :]
    cache_out_ref[...] = tail
    ext_scr[:, 0:HIST_ROWS, :] = tail

    z = _layer_norm_silu(y_scr[...].reshape(bb * tm, c) + bdw_ref[...], lng_ref[...], lnb_ref[...]).astype(BF16)
    out = _dot(z, wout_ref[...]) + bout_ref[...]
    xo_ref[...] = _gated_residual(x3, out, npost_ref[...], m[:, 2:3, :])


def _conv_layer(x, mod, cache, npre, w_in, b_in, w_dw, b_dw, ln_g, ln_b, w_out, b_out, npost, *, bb, tm):
    bsz, t, d = x.shape
    c = w_dw.shape[-1]
    assert tm % CONV_ROW_BLOCK == 0 and tm >= HIST_ROWS
    xspec = pl.BlockSpec((bb, tm, d), lambda i, j: (i, j, 0))
    cspec = pl.BlockSpec((bb, HIST_ROWS, c), lambda i, j: (i, 0, 0))
    cache32 = jnp.pad(cache, ((0, 0), (HIST_PAD, 0), (0, 0)))
    x_new, cache_new = pl.pallas_call(
        functools.partial(_conv_kernel, bb=bb, tm=tm),
        grid=(bsz // bb, t // tm),
        in_specs=[
            xspec,
            pl.BlockSpec((bb, N_MOD, d), lambda i, j: (i, 0, 0)),
            cspec,
            _resident_spec((1, d)),
            _resident_spec(w_in.shape),
            _resident_spec((1, 2 * c)),
            _resident_spec(w_dw.shape),
            _resident_spec((1, c)),
            _resident_spec((1, c)),
            _resident_spec((1, c)),
            _resident_spec(w_out.shape),
            _resident_spec((1, d)),
            _resident_spec((1, d)),
        ],
        out_specs=[xspec, cspec],
        out_shape=[jax.ShapeDtypeStruct(x.shape, F32), jax.ShapeDtypeStruct((bsz, HIST_ROWS, c), F32)],
        scratch_shapes=[
            pltpu.VMEM((bb, HIST_ROWS + tm + SUBLANES, c), F32),
            pltpu.VMEM((SUBLANES - 1, bb, HIST_ROWS + tm, c), F32),
            pltpu.VMEM((CONV_WIDTH * SUBLANES, c), F32),
            pltpu.VMEM((bb, tm, c), F32),
        ],
        compiler_params=_GRID2,
        name="conv_mixer",
    )(x, mod, cache32, npre.reshape(1, d), w_in, b_in.reshape(1, 2 * c), w_dw, b_dw.reshape(1, c),
      ln_g.reshape(1, c), ln_b.reshape(1, c), w_out, b_out.reshape(1, d), npost.reshape(1, d))
    return x_new, cache_new[:, HIST_PAD:, :]


def _ffn_kernel(x_ref, mod_ref, npre_ref, wup_ref, wdn_ref, npost_ref, xo_ref):
    x3 = x_ref[...]
    bb, tm, d = x3.shape
    m = mod_ref[...]
    hb = _modulated_norm(x3, npre_ref[...], m[:, 3:4, :], m[:, 4:5, :])
    y = jnp.zeros((bb * tm, d), F32)
    for f0 in range(0, wup_ref.shape[1], FF_TILE):
        up = _dot(hb, wup_ref[:, f0:f0 + FF_TILE])
        y = y + _dot(jnp.square(jnp.maximum(up, 0.0)).astype(BF16), wdn_ref[f0:f0 + FF_TILE, :])
    xo_ref[...] = _gated_residual(x3, y, npost_ref[...], m[:, 5:6, :])


def _ffn_layer(x, mod, npre, w_up, w_dn, npost, *, bb, tm):
    bsz, t, d = x.shape
    xspec = pl.BlockSpec((bb, tm, d), lambda i, j: (i, j, 0))
    return pl.pallas_call(
        _ffn_kernel,
        grid=(bsz // bb, t // tm),
        in_specs=[
            xspec,
            pl.BlockSpec((bb, N_MOD, d), lambda i, j: (i, 0, 0)),
            _resident_spec((1, d)),
            _resident_spec(w_up.shape),
            _resident_spec(w_dn.shape),
            _resident_spec((1, d)),
        ],
        out_specs=xspec,
        out_shape=jax.ShapeDtypeStruct(x.shape, F32),
        compiler_params=_GRID2,
        name="ffn",
    )(x, mod, npre.reshape(1, d), w_up, w_dn, npost.reshape(1, d))


def _tiling(bsz, t):
    tm = min(t, ROW_TILE)
    bb = max(1, min(bsz, ROW_TILE // tm))
    assert t % tm == 0 and bsz % bb == 0 and tm % CHUNK == 0
    return bb, tm


def _run_trunk(x, mod, gla_states, conv_caches, p):
    bb, tm = _tiling(x.shape[0], x.shape[1])
    depth = len(p['w_ffn_up'])
    new_gla, new_conv = [], []
    for i in range(depth):
        j = i // 2
        if i % 2 == 0:
            x, s = _gla_layer(x, mod[i], gla_states[j], p['norm_mix_pre'][i], p['gla_w_in'][j], p['gla_w_gate_a'][j],
                              p['gla_w_gate_b'][j], p['gla_b_gate'][j], p['gla_norm'][j], p['gla_w_out'][j],
                              p['norm_mix_post'][i], bb=bb, tm=tm)
            new_gla.append(s)
        else:
            x, s = _conv_layer(x, mod[i], conv_caches[j], p['norm_mix_pre'][i], p['conv_w_in'][j], p['conv_b_in'][j],
                               p['conv_w_dw'][j], p['conv_b_dw'][j], p['conv_ln_g'][j], p['conv_ln_b'][j],
                               p['conv_w_out'][j], p['conv_b_out'][j], p['norm_mix_post'][i], bb=bb, tm=tm)
            new_conv.append(s)
        x = _ffn_layer(x, mod[i], p['norm_ffn_pre'][i], p['w_ffn_up'][i], p['w_ffn_down'][i], p['norm_ffn_post'][i],
                       bb=bb, tm=tm)
    return x, jnp.stack(new_gla), jnp.stack(new_conv)


def kernel(x_prompt, x_sample, c_prompt, c_sample, state_gla, cache_conv, w_mod, b_mod, norm_mix_pre, norm_mix_post, norm_ffn_pre, norm_ffn_post, w_ffn_up, w_ffn_down, gla_w_in, gla_w_gate_a, gla_w_gate_b, gla_b_gate, gla_norm, gla_w_out, conv_w_in, conv_b_in, conv_w_dw, conv_b_dw, conv_ln_g, conv_ln_b, conv_w_out, conv_b_out):
    p = {
        'norm_mix_pre': norm_mix_pre, 'norm_mix_post': norm_mix_post,
        'norm_ffn_pre': norm_ffn_pre, 'norm_ffn_post': norm_ffn_post,
        'w_ffn_up': [w.astype(BF16) for w in w_ffn_up], 'w_ffn_down': [w.astype(BF16) for w in w_ffn_down],
        'gla_w_in': gla_w_in, 'gla_w_gate_a': gla_w_gate_a.astype(BF16),
        'gla_w_gate_b': gla_w_gate_b.astype(BF16), 'gla_b_gate': gla_b_gate, 'gla_norm': gla_norm,
        'gla_w_out': gla_w_out,
        'conv_w_in': conv_w_in.astype(BF16), 'conv_b_in': conv_b_in, 'conv_w_dw': conv_w_dw, 'conv_b_dw': conv_b_dw,
        'conv_ln_g': conv_ln_g, 'conv_ln_b': conv_ln_b, 'conv_w_out': conv_w_out.astype(BF16),
        'conv_b_out': conv_b_out,
    }
    depth, d, _ = w_mod.shape
    b_p, b_s = x_prompt.shape[0], x_sample.shape[0]
    n_gla, _, heads, dk, dv = state_gla.shape
    n_conv, _, hist, cdim = cache_conv.shape

    c_all = jnp.concatenate([c_prompt, c_sample, jnp.zeros((MOD_ROWS - b_p - b_s, d), F32)], axis=0)
    mod = _modulation(c_all, w_mod, b_mod)
    mod_p = mod[:, :b_p].reshape(depth, b_p, N_MOD, d)
    mod_s = mod[:, b_p:b_p + b_s].reshape(depth, b_s, N_MOD, d)

    gla0 = jnp.zeros((n_gla, b_p, heads, dk, dv), F32)
    conv0 = jnp.zeros((n_conv, b_p, hist, cdim), F32)
    y_p, gla_p, conv_p = _run_trunk(x_prompt, mod_p, gla0, conv0, p)
    y_s, gla_s, conv_s = _run_trunk(x_sample, mod_s, state_gla, cache_conv, p)
    return (y_p, y_s, gla_p, conv_p, gla_s, conv_s)
```

```python
import functools

import jax
import jax.numpy as jnp
from jax import lax
from jax.experimental import pallas as pl
from jax.experimental.pallas import tpu as pltpu

F32 = jnp.float32
BF16 = jnp.bfloat16

CHUNK = 64
GLA_HEADS = 4
GATE_TAU = 16.0
CONV_WIDTH = 31
N_MOD = 6
EPS = 1e-6

SUBLANES = 8
HIST_ROWS = 32
HIST_PAD = HIST_ROWS - (CONV_WIDTH - 1)
CONV_ROW_BLOCK = 64
CONV_LANE_BLOCK = 128
SHIFT_ROW_BLOCK = 32
FF_TILE = 1024
ROW_TILE = 512
VMEM_LIMIT_BYTES = 56 * 1024 * 1024
MOD_ROWS = 16
MOD_COL_TILE = 1536


def _sigmoid(x):
    return 1.0 / (1.0 + jnp.exp(-x))


def _silu(x):
    return x * _sigmoid(x)


def _log_sigmoid(x):
    return jnp.minimum(x, 0.0) - jnp.log(1.0 + jnp.exp(-jnp.abs(x)))


def _rms(x, g):
    return x * lax.rsqrt(jnp.mean(x * x, axis=-1, keepdims=True) + EPS) * g


def _layer_norm_silu(y, g, b):
    mu = jnp.mean(y, axis=-1, keepdims=True)
    yc = y - mu
    var = jnp.mean(yc * yc, axis=-1, keepdims=True)
    return _silu(yc * lax.rsqrt(var + EPS) * g + b)


def _dot(a, b):
    return jnp.dot(a, b, preferred_element_type=F32)


def _modulated_norm(x3, g, shift, scale):
    bb, tm, d = x3.shape
    h = _rms(x3.reshape(bb * tm, d), g).reshape(bb, tm, d) * (1.0 + scale) + shift
    return h.reshape(bb * tm, d).astype(BF16)


def _gated_residual(x3, y2, g, gate):
    bb, tm, d = x3.shape
    return x3 + gate * _rms(y2, g).reshape(bb, tm, d)


def _resident_spec(shape):
    nd = len(shape)
    return pl.BlockSpec(shape, lambda *_: (0,) * nd, pipeline_mode=pl.Buffered(1))


_GRID2 = pltpu.CompilerParams(dimension_semantics=("arbitrary", "arbitrary"), vmem_limit_bytes=VMEM_LIMIT_BYTES)


def _mod_kernel(c_ref, w_ref, b_ref, o_ref):
    s = _silu(c_ref[...]).astype(BF16)
    o_ref[0] = _dot(s, w_ref[0].astype(BF16)) + b_ref[0]


def _modulation(c_all, w_mod, b_mod):
    depth, d, n = w_mod.shape
    return pl.pallas_call(
        _mod_kernel,
        grid=(depth, n // MOD_COL_TILE),
        in_specs=[
            pl.BlockSpec((MOD_ROWS, d), lambda i, j: (0, 0)),
            pl.BlockSpec((1, d, MOD_COL_TILE), lambda i, j: (i, 0, j)),
            pl.BlockSpec((1, 1, MOD_COL_TILE), lambda i, j: (i, 0, j)),
        ],
        out_specs=pl.BlockSpec((1, MOD_ROWS, MOD_COL_TILE), lambda i, j: (i, 0, j)),
        out_shape=jax.ShapeDtypeStruct((depth, MOD_ROWS, n), F32),
        compiler_params=_GRID2,
        name="modulation",
    )(c_all, w_mod, b_mod.reshape(depth, 1, n))


def _chunk_cumsum(g, bcum_scr):
    m, lanes = g.shape
    x = g.reshape(m // SUBLANES, SUBLANES, lanes)
    sub = lax.broadcasted_iota(jnp.int32, (1, SUBLANES, lanes), 1)
    s = 1
    while s < SUBLANES:
        x = x + jnp.where(sub >= s, pltpu.roll(x, s, axis=1), 0.0)
        s *= 2
    groups = CHUNK // SUBLANES
    totals = []
    for ci in range(m // CHUNK):
        carry = None
        for gi in range(groups):
            idx = ci * groups + gi
            blk = x[idx] if carry is None else x[idx] + carry
            bcum_scr[idx * SUBLANES:(idx + 1) * SUBLANES, :] = blk
            carry = jnp.broadcast_to(blk[SUBLANES - 1:SUBLANES, :], (SUBLANES, lanes))
        totals.append(carry)
    return totals


def _gla_kernel(x_ref, mod_ref, s0_ref, npre_ref, win_ref, wga_ref, wgb_ref, bg_ref, gn_ref, wout_ref,
                npost_ref, xo_ref, st_ref, proj_scr, bcum_scr, qd_scr, ki_scr, ke_scr, dexp_scr, o_scr, u_scr, sc_scr,
                *, bb, tm):
    kdim = wgb_ref.shape[-1]
    dk = kdim // GLA_HEADS
    vdim = wout_ref.shape[0]
    dv = vdim // GLA_HEADS
    rows_total = bb * tm
    n_chunks = tm // CHUNK
    chunks = [slice(ci * CHUNK, (ci + 1) * CHUNK) for ci in range(rows_total // CHUNK)]
    heads = [(slice(h * dk, (h + 1) * dk), slice(h * dv, (h + 1) * dv)) for h in range(GLA_HEADS)]

    @pl.when(pl.program_id(1) == 0)
    def _():
        st_ref[...] = s0_ref[...]
        dexp_scr[...] = jnp.zeros(dexp_scr.shape, F32)

    x3 = x_ref[...]
    m = mod_ref[...]
    hb = _modulated_norm(x3, npre_ref[...], m[:, 0:1, :], m[:, 1:2, :])
    ga = _dot(hb, wga_ref[...]).astype(BF16)
    r_cols = slice(2 * kdim + vdim, 2 * kdim + 2 * vdim)
    proj_scr[:, r_cols] = _dot(hb, win_ref[:, r_cols].astype(BF16))
    gl = _dot(ga, wgb_ref[...]) + bg_ref[...]
    totals = _chunk_cumsum(_log_sigmoid(gl) / GATE_TAU, bcum_scr)
    proj_scr[:, 0:2 * kdim + vdim] = _dot(hb, win_ref[:, 0:2 * kdim + vdim].astype(BF16))

    q_scale = dk ** -0.5
    for ci, rows in enumerate(chunks):
        bcum = bcum_scr[rows, :]
        blast = totals[ci][0:1, :]
        q = proj_scr[rows, 0:kdim]
        k = proj_scr[rows, kdim:2 * kdim]
        qd_scr[rows, :] = ((q * q_scale) * jnp.exp(bcum)).astype(BF16)
        ki_scr[rows, :] = (k * jnp.exp(-bcum)).astype(BF16)
        ke_scr[rows, :] = k * jnp.exp(blast - bcum)
        dexp_scr[ci:ci + 1, :] = jnp.exp(blast)
    decay_cols = dexp_scr[...].T

    row = lax.broadcasted_iota(jnp.int32, (CHUNK, CHUNK), 0)
    col = lax.broadcasted_iota(jnp.int32, (CHUNK, CHUNK), 1)
    causal = col <= row
    for ci, rows in enumerate(chunks):
        for h, (ks, vs) in enumerate(heads):
            scores = lax.dot_general(qd_scr[rows, ks], ki_scr[rows, ks], (((1,), (1,)), ((), ())),
                                     preferred_element_type=F32)
            sc_scr[ci, h] = jnp.where(causal, scores, 0.0).astype(BF16)
    for ci, rows in enumerate(chunks):
        for h, (ks, vs) in enumerate(heads):
            vh = proj_scr[rows, 2 * kdim + h * dv:2 * kdim + (h + 1) * dv].astype(BF16)
            keh_t = ke_scr[rows, ks].T.astype(BF16)
            both = _dot(jnp.concatenate([sc_scr[ci, h], keh_t], axis=0), vh)
            o_scr[rows, vs] = both[:CHUNK]
            u_scr[ci, h] = both[CHUNK:]

    gn = gn_ref[...]
    for ci, rows in enumerate(chunks):
        b = ci // n_chunks
        for h, (ks, vs) in enumerate(heads):
            s = st_ref[b, h]
            o = o_scr[rows, vs] + _dot(qd_scr[rows, ks], s.astype(BF16))
            o_scr[rows, vs] = _rms(o, gn)
            st_ref[b, h] = decay_cols[ks, ci:ci + 1] * s + u_scr[ci, h]

    r = proj_scr[:, 2 * kdim + vdim:]
    z = (o_scr[...] * _silu(r)).astype(BF16)
    y = _dot(z, wout_ref[...].astype(BF16))
    xo_ref[...] = _gated_residual(x3, y, npost_ref[...], m[:, 2:3, :])


def _gla_layer(x, mod, s0, npre, w_in, w_ga, w_gb, b_g, gnorm, w_out, npost, *, bb, tm):
    bsz, t, d = x.shape
    _, heads, dk, dv = s0.shape
    pdim = w_in.shape[1]
    kdim, vdim = heads * dk, heads * dv
    rows = bb * tm
    chunks_pad = -(-(rows // CHUNK) // SUBLANES) * SUBLANES
    xspec = pl.BlockSpec((bb, tm, d), lambda i, j: (i, j, 0))
    sspec = pl.BlockSpec((bb, heads, dk, dv), lambda i, j: (i, 0, 0, 0))
    return pl.pallas_call(
        functools.partial(_gla_kernel, bb=bb, tm=tm),
        grid=(bsz // bb, t // tm),
        in_specs=[
            xspec,
            pl.BlockSpec((bb, N_MOD, d), lambda i, j: (i, 0, 0)),
            sspec,
            _resident_spec((1, d)),
            _resident_spec((d, pdim)),
            _resident_spec(w_ga.shape),
            _resident_spec(w_gb.shape),
            _resident_spec((1, kdim)),
            _resident_spec((1, dv)),
            _resident_spec((vdim, d)),
            _resident_spec((1, d)),
        ],
        out_specs=[xspec, sspec],
        out_shape=[jax.ShapeDtypeStruct(x.shape, F32), jax.ShapeDtypeStruct(s0.shape, F32)],
        scratch_shapes=[
            pltpu.VMEM((rows, pdim), F32),
            pltpu.VMEM((rows, kdim), F32),
            pltpu.VMEM((rows, kdim), BF16),
            pltpu.VMEM((rows, kdim), BF16),
            pltpu.VMEM((rows, kdim), F32),
            pltpu.VMEM((chunks_pad, kdim), F32),
            pltpu.VMEM((rows, vdim), F32),
            pltpu.VMEM((rows // CHUNK, heads, dk, dv), F32),
            pltpu.VMEM((rows // CHUNK, heads, CHUNK, CHUNK), BF16),
        ],
        compiler_params=_GRID2,
        name="gla_mixer",
    )(x, mod, s0, npre.reshape(1, d), w_in, w_ga, w_gb, b_g.reshape(1, kdim), gnorm.reshape(1, dv), w_out,
      npost.reshape(1, d))


def _conv_kernel(x_ref, mod_ref, cache_ref, npre_ref, win_ref, bin_ref, wdw_ref, bdw_ref, lng_ref, lnb_ref,
                 wout_ref, bout_ref, npost_ref, xo_ref, cache_out_ref, ext_scr, shift_scr, wtap_scr, y_scr,
                 *, bb, tm):
    c = wdw_ref.shape[-1]
    ext_rows = HIST_ROWS + tm

    @pl.when(pl.program_id(1) == 0)
    def _():
        ext_scr[:, 0:HIST_ROWS, :] = cache_ref[...]
        ext_scr[:, ext_rows:, :] = jnp.zeros((bb, SUBLANES, c), F32)
        for j in range(CONV_WIDTH):
            wtap_scr[j * SUBLANES:(j + 1) * SUBLANES, :] = jnp.broadcast_to(wdw_ref[j:j + 1, :], (SUBLANES, c))

    x3 = x_ref[...]
    m = mod_ref[...]
    hb = _modulated_norm(x3, npre_ref[...], m[:, 0:1, :], m[:, 1:2, :])
    gate = _sigmoid(_dot(hb, win_ref[:, c:]) + bin_ref[:, c:])
    glu = (_dot(hb, win_ref[:, :c]) + bin_ref[:, :c]) * gate
    ext_scr[:, HIST_ROWS:ext_rows, :] = glu.reshape(bb, tm, c)

    n_shift_blocks = ext_rows // SHIFT_ROW_BLOCK

    def shift_step(idx, carry):
        b = idx // n_shift_blocks
        r0 = pl.multiple_of((idx % n_shift_blocks) * SHIFT_ROW_BLOCK, SHIFT_ROW_BLOCK)
        groups = SHIFT_ROW_BLOCK // SUBLANES
        window = ext_scr[b, pl.ds(r0, SHIFT_ROW_BLOCK + SUBLANES), :].reshape(groups + 1, SUBLANES, c)
        sub = lax.broadcasted_iota(jnp.int32, (1, SUBLANES, c), 1)
        for s in range(1, SUBLANES):
            mixed = jnp.where(sub >= s, window[:groups], window[1:])
            shifted = pltpu.roll(mixed, SUBLANES - s, axis=1)
            shift_scr[s - 1, b, pl.ds(r0, SHIFT_ROW_BLOCK), :] = shifted.reshape(SHIFT_ROW_BLOCK, c)
        return carry

    lax.fori_loop(0, bb * n_shift_blocks, shift_step, 0)

    n_row_blocks = tm // CONV_ROW_BLOCK

    def tap_step(idx, carry):
        b = idx // n_row_blocks
        r0 = pl.multiple_of((idx % n_row_blocks) * CONV_ROW_BLOCK, CONV_ROW_BLOCK)
        groups = CONV_ROW_BLOCK // SUBLANES
        for l0 in range(0, c, CONV_LANE_BLOCK):
            lanes = slice(l0, l0 + CONV_LANE_BLOCK)
            acc = jnp.zeros((CONV_ROW_BLOCK, CONV_LANE_BLOCK), F32)
            for s in range(SUBLANES):
                taps = [j for j in range(CONV_WIDTH) if (HIST_PAD + j) % SUBLANES == s]
                first = (HIST_PAD + taps[0]) // SUBLANES * SUBLANES
                span = (HIST_PAD + taps[-1]) // SUBLANES * SUBLANES - first + CONV_ROW_BLOCK
                rows = pl.ds(r0 + first, span)
                window = ext_scr[b, rows, lanes] if s == 0 else shift_scr[s - 1, b, rows, lanes]
                for j in taps:
                    a = (HIST_PAD + j) // SUBLANES * SUBLANES - first
                    w8 = wtap_scr[j * SUBLANES:(j + 1) * SUBLANES, lanes]
                    src = window[a:a + CONV_ROW_BLOCK].reshape(groups, SUBLANES, CONV_LANE_BLOCK)
                    acc = acc + (w8[None] * src).reshape(CONV_ROW_BLOCK, CONV_LANE_BLOCK)
            y_scr[b, pl.ds(r0, CONV_ROW_BLOCK), lanes] = acc
        return carry

    lax.fori_loop(0, bb * n_row_blocks, tap_step, 0)

    tail = ext_scr[:, tm:ext_rows, :]
    cache_out_ref[...] = tail
    ext_scr[:, 0:HIST_ROWS, :] = tail

    z = _layer_norm_silu(y_scr[...].reshape(bb * tm, c) + bdw_ref[...], lng_ref[...], lnb_ref[...]).astype(BF16)
    out = _dot(z, wout_ref[...]) + bout_ref[...]
    xo_ref[...] = _gated_residual(x3, out, npost_ref[...], m[:, 2:3, :])


def _conv_layer(x, mod, cache, npre, w_in, b_in, w_dw, b_dw, ln_g, ln_b, w_out, b_out, npost, *, bb, tm):
    bsz, t, d = x.shape
    c = w_dw.shape[-1]
    assert tm % CONV_ROW_BLOCK == 0 and tm >= HIST_ROWS
    xspec = pl.BlockSpec((bb, tm, d), lambda i, j: (i, j, 0))
    cspec = pl.BlockSpec((bb, HIST_ROWS, c), lambda i, j: (i, 0, 0))
    cache32 = jnp.pad(cache, ((0, 0), (HIST_PAD, 0), (0, 0)))
    x_new, cache_new = pl.pallas_call(
        functools.partial(_conv_kernel, bb=bb, tm=tm),
        grid=(bsz // bb, t // tm),
        in_specs=[
            xspec,
            pl.BlockSpec((bb, N_MOD, d), lambda i, j: (i, 0, 0)),
            cspec,
            _resident_spec((1, d)),
            _resident_spec(w_in.shape),
            _resident_spec((1, 2 * c)),
            _resident_spec(w_dw.shape),
            _resident_spec((1, c)),
            _resident_spec((1, c)),
            _resident_spec((1, c)),
            _resident_spec(w_out.shape),
            _resident_spec((1, d)),
            _resident_spec((1, d)),
        ],
        out_specs=[xspec, cspec],
        out_shape=[jax.ShapeDtypeStruct(x.shape, F32), jax.ShapeDtypeStruct((bsz, HIST_ROWS, c), F32)],
        scratch_shapes=[
            pltpu.VMEM((bb, HIST_ROWS + tm + SUBLANES, c), F32),
            pltpu.VMEM((SUBLANES - 1, bb, HIST_ROWS + tm, c), F32),
            pltpu.VMEM((CONV_WIDTH * SUBLANES, c), F32),
            pltpu.VMEM((bb, tm, c), F32),
        ],
        compiler_params=_GRID2,
        name="conv_mixer",
    )(x, mod, cache32, npre.reshape(1, d), w_in, b_in.reshape(1, 2 * c), w_dw, b_dw.reshape(1, c),
      ln_g.reshape(1, c), ln_b.reshape(1, c), w_out, b_out.reshape(1, d), npost.reshape(1, d))
    return x_new, cache_new[:, HIST_PAD:, :]


def _ffn_kernel(x_ref, mod_ref, npre_ref, wup_ref, wdn_ref, npost_ref, xo_ref):
    x3 = x_ref[...]
    bb, tm, d = x3.shape
    m = mod_ref[...]
    hb = _modulated_norm(x3, npre_ref[...], m[:, 3:4, :], m[:, 4:5, :])
    y = jnp.zeros((bb * tm, d), F32)
    for f0 in range(0, wup_ref.shape[1], FF_TILE):
        up = _dot(hb, wup_ref[:, f0:f0 + FF_TILE])
        y = y + _dot(jnp.square(jnp.maximum(up, 0.0)).astype(BF16), wdn_ref[f0:f0 + FF_TILE, :])
    xo_ref[...] = _gated_residual(x3, y, npost_ref[...], m[:, 5:6, :])


def _ffn_layer(x, mod, npre, w_up, w_dn, npost, *, bb, tm):
    bsz, t, d = x.shape
    xspec = pl.BlockSpec((bb, tm, d), lambda i, j: (i, j, 0))
    return pl.pallas_call(
        _ffn_kernel,
        grid=(bsz // bb, t // tm),
        in_specs=[
            xspec,
            pl.BlockSpec((bb, N_MOD, d), lambda i, j: (i, 0, 0)),
            _resident_spec((1, d)),
            _resident_spec(w_up.shape),
            _resident_spec(w_dn.shape),
            _resident_spec((1, d)),
        ],
        out_specs=xspec,
        out_shape=jax.ShapeDtypeStruct(x.shape, F32),
        compiler_params=_GRID2,
        name="ffn",
    )(x, mod, npre.reshape(1, d), w_up, w_dn, npost.reshape(1, d))


def _tiling(bsz, t):
    tm = min(t, ROW_TILE)
    bb = max(1, min(bsz, ROW_TILE // tm))
    assert t % tm == 0 and bsz % bb == 0 and tm % CHUNK == 0
    return bb, tm


def _run_trunk(x, mod, gla_states, conv_caches, p):
    bb, tm = _tiling(x.shape[0], x.shape[1])
    depth = len(p['w_ffn_up'])
    new_gla, new_conv = [], []
    for i in range(depth):
        j = i // 2
        if i % 2 == 0:
            x, s = _gla_layer(x, mod[i], gla_states[j], p['norm_mix_pre'][i], p['gla_w_in'][j], p['gla_w_gate_a'][j],
                              p['gla_w_gate_b'][j], p['gla_b_gate'][j], p['gla_norm'][j], p['gla_w_out'][j],
                              p['norm_mix_post'][i], bb=bb, tm=tm)
            new_gla.append(s)
        else:
            x, s = _conv_layer(x, mod[i], conv_caches[j], p['norm_mix_pre'][i], p['conv_w_in'][j], p['conv_b_in'][j],
                               p['conv_w_dw'][j], p['conv_b_dw'][j], p['conv_ln_g'][j], p['conv_ln_b'][j],
                               p['conv_w_out'][j], p['conv_b_out'][j], p['norm_mix_post'][i], bb=bb, tm=tm)
            new_conv.append(s)
        x = _ffn_layer(x, mod[i], p['norm_ffn_pre'][i], p['w_ffn_up'][i], p['w_ffn_down'][i], p['norm_ffn_post'][i],
                       bb=bb, tm=tm)
    return x, jnp.stack(new_gla), jnp.stack(new_conv)


def kernel(x_prompt, x_sample, c_prompt, c_sample, state_gla, cache_conv, w_mod, b_mod, norm_mix_pre, norm_mix_post, norm_ffn_pre, norm_ffn_post, w_ffn_up, w_ffn_down, gla_w_in, gla_w_gate_a, gla_w_gate_b, gla_b_gate, gla_norm, gla_w_out, conv_w_in, conv_b_in, conv_w_dw, conv_b_dw, conv_ln_g, conv_ln_b, conv_w_out, conv_b_out):
    p = {
        'norm_mix_pre': norm_mix_pre, 'norm_mix_post': norm_mix_post,
        'norm_ffn_pre': norm_ffn_pre, 'norm_ffn_post': norm_ffn_post,
        'w_ffn_up': [w.astype(BF16) for w in w_ffn_up], 'w_ffn_down': [w.astype(BF16) for w in w_ffn_down],
        'gla_w_in': gla_w_in, 'gla_w_gate_a': gla_w_gate_a.astype(BF16),
        'gla_w_gate_b': gla_w_gate_b.astype(BF16), 'gla_b_gate': gla_b_gate, 'gla_norm': gla_norm,
        'gla_w_out': gla_w_out,
        'conv_w_in': conv_w_in.astype(BF16), 'conv_b_in': conv_b_in, 'conv_w_dw': conv_w_dw, 'conv_b_dw': conv_b_dw,
        'conv_ln_g': conv_ln_g, 'conv_ln_b': conv_ln_b, 'conv_w_out': conv_w_out.astype(BF16),
        'conv_b_out': conv_b_out,
    }
    depth, d, _ = w_mod.shape
    b_p, b_s = x_prompt.shape[0], x_sample.shape[0]
    n_gla, _, heads, dk, dv = state_gla.shape
    n_conv, _, hist, cdim = cache_conv.shape

    c_all = jnp.concatenate([c_prompt, c_sample, jnp.zeros((MOD_ROWS - b_p - b_s, d), F32)], axis=0)
    mod = _modulation(c_all, w_mod, b_mod)
    mod_p = mod[:, :b_p].reshape(depth, b_p, N_MOD, d)
    mod_s = mod[:, b_p:b_p + b_s].reshape(depth, b_s, N_MOD, d)

    gla0 = jnp.zeros((n_gla, b_p, heads, dk, dv), F32)
    conv0 = jnp.zeros((n_conv, b_p, hist, cdim), F32)
    y_p, gla_p, conv_p = _run_trunk(x_prompt, mod_p, gla0, conv0, p)
    y_s, gla_s, conv_s = _run_trunk(x_sample, mod_s, state_gla, cache_conv, p)
    return (y_p, y_s, gla_p, conv_p, gla_s, conv_s)
```
